```python
import math
import jax
import jax.numpy as jnp
from jax import lax
import numpy as np

D_MODEL = 1024
BATCH = 2
SEQ = 8192
DEPTH = 4

N_A_LAYERS = DEPTH // 2
N_B_LAYERS = DEPTH - N_A_LAYERS

A_HEADS = 8
A_DK = 128
A_DV = 128
A_KEY_W = A_HEADS * A_DK
A_VAL_W = A_HEADS * A_DV
A_CONV_CH = 2 * A_KEY_W + A_VAL_W
A_IN_W = A_CONV_CH + A_VAL_W + 2 * A_HEADS
CONV_W = 4
CHUNK = 64

B_HEADS = 8
B_DH = 128
B_W = B_HEADS * B_DH
MOBA_BLOCK = 256
MOBA_TOPK = 3
Q_SUB = 32

N_BUCKETS = 32
MAX_DIST = 2048
EPS = 1e-6

kernel_name = "hybrid_gdn_moba_yoco"


def rms_norm(x, g):
    xf = x.astype(jnp.float32)
    y = xf * lax.rsqrt(jnp.mean(xf * xf, axis=-1, keepdims=True) + EPS)
    return (y * g.astype(jnp.float32)).astype(x.dtype)


def l2_normalize(t):
    tf = t.astype(jnp.float32)
    return tf * lax.rsqrt(jnp.sum(tf * tf, axis=-1, keepdims=True) + EPS)


def causal_conv_silu(u, w):
    c = u.shape[-1]
    up = jnp.pad(u, ((0, 0), (CONV_W - 1, 0), (0, 0)))
    out = lax.conv_general_dilated(up, w[:, None, :].astype(u.dtype), window_strides=(1,), padding="VALID", dimension_numbers=("NWC", "WIO", "NWC"), feature_group_count=c)
    return jax.nn.silu(out)


def gated_delta_rule(q, k, v, g, beta):
    bsz, s, h, dk = q.shape
    dv = v.shape[-1]
    n = s // CHUNK
    f32 = jnp.float32

    def to_chunks(t):
        t = t.astype(f32).reshape((bsz, n, CHUNK) + t.shape[2:])
        return jnp.moveaxis(t, 3, 1)

    q, k, v, g, beta = (to_chunks(t) for t in (q, k, v, g, beta))
    g = jnp.cumsum(g, axis=-1)
    idx = jnp.arange(CHUNK)
    tril = idx[:, None] >= idx[None, :]
    strict = idx[:, None] > idx[None, :]
    decay = jnp.exp(jnp.where(tril, g[..., :, None] - g[..., None, :], -jnp.inf))
    k_beta = k * beta[..., None]
    a_mat = jnp.where(strict, jnp.einsum("bhncd,bhnsd->bhncs", k_beta, k) * decay, 0.0)
    rhs = jnp.concatenate([v * beta[..., None], k_beta * jnp.exp(g)[..., None]], axis=-1)
    sol = lax.linalg.triangular_solve(a_mat + jnp.eye(CHUNK, dtype=f32), rhs, left_side=True, lower=True, unit_diagonal=True)
    u, w = sol[..., :dv], sol[..., dv:]
    qk_intra = jnp.where(tril, jnp.einsum("bhncd,bhnsd->bhncs", q, k) * decay, 0.0)
    q_dec = q * jnp.exp(g)[..., None]
    k_dec = k * jnp.exp(g[..., -1:] - g)[..., None]
    g_last = jnp.exp(g[..., -1])

    def step(state, xs):
        u_c, w_c, qd_c, qk_c, kd_c, gl_c = xs
        v_new = u_c - jnp.einsum("bhck,bhkv->bhcv", w_c, state)
        o = jnp.einsum("bhck,bhkv->bhcv", qd_c, state) + jnp.einsum("bhcs,bhsv->bhcv", qk_c, v_new)
        state = state * gl_c[..., None, None] + jnp.einsum("bhck,bhcv->bhkv", kd_c, v_new)
        return state, o

    xs = tuple(jnp.moveaxis(t, 2, 0) for t in (u, w, q_dec, qk_intra, k_dec, g_last))
    state0 = jnp.zeros((bsz, h, dk, dv), f32)
    _, o = lax.scan(step, state0, xs)
    return o.transpose(1, 0, 3, 2, 4).reshape(bsz, s, h, dv)


def gated_deltanet_mixer(hn, w_in, conv_w, a_log, dt_bias, out_norm_g, w_out):
    bsz, s, _ = hn.shape
    f32 = jnp.float32
    proj = hn @ w_in
    qkv = causal_conv_silu(proj[..., :A_CONV_CH], conv_w)
    z = proj[..., A_CONV_CH:A_CONV_CH + A_VAL_W]
    a = proj[..., A_CONV_CH + A_VAL_W:A_CONV_CH + A_VAL_W + A_HEADS]
    b = proj[..., A_CONV_CH + A_VAL_W + A_HEADS:]
    q = l2_normalize(qkv[..., :A_KEY_W].reshape(bsz, s, A_HEADS, A_DK)) * (A_DK ** -0.5)
    k = l2_normalize(qkv[..., A_KEY_W:2 * A_KEY_W].reshape(bsz, s, A_HEADS, A_DK))
    v = qkv[..., 2 * A_KEY_W:].reshape(bsz, s, A_HEADS, A_DV)
    g = -jnp.exp(a_log.astype(f32)) * jax.nn.softplus(a.astype(f32) + dt_bias.astype(f32))
    beta = jax.nn.sigmoid(b.astype(f32))
    o = gated_delta_rule(q, k, v, g, beta)
    o = rms_norm(o, out_norm_g) * jax.nn.silu(z.astype(f32)).reshape(bsz, s, A_HEADS, A_DV)
    return o.reshape(bsz, s, A_VAL_W).astype(hn.dtype) @ w_out


def t5_bucket(rel):
    n = jnp.maximum(rel, 0)
    max_exact = N_BUCKETS // 2
    nf = jnp.maximum(n, 1).astype(jnp.float32)
    large = max_exact + (jnp.log(nf / max_exact) / math.log(MAX_DIST / max_exact) * (N_BUCKETS - max_exact)).astype(jnp.int32)
    large = jnp.minimum(large, N_BUCKETS - 1)
    return jnp.where(n < max_exact, n, large)


def shared_kv(x, kv_norm_g, w_kv):
    bsz, s, _ = x.shape
    kv = rms_norm(x, kv_norm_g) @ w_kv
    n_blk = -(-s // MOBA_BLOCK)
    s_pad = n_blk * MOBA_BLOCK

    def blocks(t):
        t = jnp.pad(t.reshape(bsz, s, B_HEADS, B_DH), ((0, 0), (0, s_pad - s), (0, 0), (0, 0)))
        return t.reshape(bsz, n_blk, MOBA_BLOCK, B_HEADS, B_DH).transpose(0, 3, 1, 2, 4)

    kb = blocks(kv[..., :B_W])
    vb = blocks(kv[..., B_W:])
    k_mean = jnp.mean(kb.astype(jnp.float32), axis=3)
    return kb, vb, k_mean


def moba_attention(q, kb, vb, k_mean, rel_bias):
    bsz, s, h, d = q.shape
    n_blk = kb.shape[2]
    topk = min(MOBA_TOPK, n_blk)
    scale = d ** -0.5
    f32 = jnp.float32
    n_sub = s // Q_SUB
    q_sub = q.transpose(0, 2, 1, 3).reshape(bsz, h, n_sub, Q_SUB, d).transpose(2, 0, 1, 3, 4)
    bias_t = rel_bias.T.astype(f32)
    b_ix = jnp.arange(bsz)[:, None, None, None]
    h_ix = jnp.arange(h)[None, :, None, None]
    blk_ids = jnp.arange(n_blk)
    offs = jnp.arange(MOBA_BLOCK)

    def one_sub(args):
        qs, s_idx = args
        q_pos = s_idx * Q_SUB + jnp.arange(Q_SUB)
        cur = (s_idx * Q_SUB) // MOBA_BLOCK
        qf = qs.astype(f32)
        gate = jnp.einsum("bhqd,bhnd->bhqn", qf, k_mean)
        gate = jnp.where(blk_ids < cur, gate, -jnp.inf)
        _, sel = lax.top_k(gate, topk)
        sel_ok = jnp.arange(topk) < cur
        k_sel = kb[b_ix, h_ix, sel].astype(f32)
        v_sel = vb[b_ix, h_ix, sel].astype(f32)
        k_pos_sel = sel[..., None] * MOBA_BLOCK + offs
        s_sel = jnp.einsum("bhqd,bhqkjd->bhqkj", qf, k_sel) * scale
        s_sel = s_sel + bias_t[h_ix[..., None], t5_bucket(q_pos[:, None, None] - k_pos_sel)]
        s_sel = jnp.where(sel_ok[:, None], s_sel, -jnp.inf).reshape(bsz, h, Q_SUB, topk * MOBA_BLOCK)
        k_own = lax.dynamic_index_in_dim(kb, cur, axis=2, keepdims=False).astype(f32)
        v_own = lax.dynamic_index_in_dim(vb, cur, axis=2, keepdims=False).astype(f32)
        rel_own = q_pos[:, None] - (cur * MOBA_BLOCK + offs)[None, :]
        s_own = jnp.einsum("bhqd,bhjd->bhqj", qf, k_own) * scale + bias_t[:, t5_bucket(rel_own)]
        s_own = jnp.where(rel_own >= 0, s_own, -jnp.inf)
        p = jax.nn.softmax(jnp.concatenate([s_sel, s_own], axis=-1), axis=-1)
        p_sel = p[..., :topk * MOBA_BLOCK].reshape(bsz, h, Q_SUB, topk, MOBA_BLOCK)
        p_own = p[..., topk * MOBA_BLOCK:]
        o = jnp.einsum("bhqkj,bhqkjd->bhqd", p_sel, v_sel) + jnp.einsum("bhqj,bhjd->bhqd", p_own, v_own)
        return o.astype(qs.dtype)

    o = lax.map(one_sub, (q_sub, jnp.arange(n_sub)))
    return o.transpose(1, 0, 3, 2, 4).reshape(bsz, s, h * d)


def moba_mixer(hn, w_in, w_out, kb, vb, k_mean, rel_bias):
    bsz, s, _ = hn.shape
    proj = hn @ w_in
    q = proj[..., :B_W].reshape(bsz, s, B_HEADS, B_DH)
    z = proj[..., B_W:]
    o = moba_attention(q, kb, vb, k_mean, rel_bias)
    return (o * jax.nn.silu(z)).astype(hn.dtype) @ w_out


def setup_inputs(seed: int = 0) -> dict:
    key = jax.random.key(seed)
    ks = jax.random.split(key, 16)
    f32 = jnp.float32
    na, nb = N_A_LAYERS, N_B_LAYERS

    def gain(k, shape):
        return 1.0 + 0.01 * jax.random.normal(k, shape, f32)

    x = jax.random.normal(ks[0], (BATCH, SEQ, D_MODEL), f32)
    a_norm_g = gain(ks[1], (na, D_MODEL))
    a_w_in = jax.random.normal(ks[2], (na, D_MODEL, A_IN_W), f32) * D_MODEL ** -0.5
    a_conv_w = jax.random.normal(ks[3], (na, CONV_W, A_CONV_CH), f32) * CONV_W ** -0.5
    a_log = jnp.log(jax.random.uniform(ks[4], (na, A_HEADS), f32, 1.0, 16.0))
    dt = jnp.exp(jax.random.uniform(ks[5], (na, A_HEADS), f32, math.log(1e-3), math.log(1e-1)))
    a_dt_bias = dt + jnp.log(-jnp.expm1(-dt))
    a_out_norm_g = gain(ks[6], (na, A_DV))
    a_w_out = jax.random.normal(ks[7], (na, A_VAL_W, D_MODEL), f32) * A_VAL_W ** -0.5
    kv_norm_g = gain(ks[8], (D_MODEL,))
    w_kv = jax.random.normal(ks[9], (D_MODEL, 2 * B_W), f32) * D_MODEL ** -0.5
    b_norm_g = gain(ks[10], (nb, D_MODEL))
    b_w_in = jax.random.normal(ks[11], (nb, D_MODEL, 2 * B_W), f32) * D_MODEL ** -0.5
    b_w_out = jax.random.normal(ks[12], (nb, B_W, D_MODEL), f32) * B_W ** -0.5
    rel_bias = 0.2 * jax.random.normal(ks[13], (N_BUCKETS, B_HEADS), f32)
    final_norm_g = gain(ks[14], (D_MODEL,))
    return {"x": x, "a_norm_g": a_norm_g, "a_w_in": a_w_in, "a_conv_w": a_conv_w, "a_log": a_log, "a_dt_bias": a_dt_bias, "a_out_norm_g": a_out_norm_g, "a_w_out": a_w_out, "kv_norm_g": kv_norm_g, "w_kv": w_kv, "b_norm_g": b_norm_g, "b_w_in": b_w_in, "b_w_out": b_w_out, "rel_bias": rel_bias, "final_norm_g": final_norm_g}


def reference(x, a_norm_g, a_w_in, a_conv_w, a_log, a_dt_bias, a_out_norm_g, a_w_out, kv_norm_g, w_kv, b_norm_g, b_w_in, b_w_out, rel_bias, final_norm_g):
    shared = None
    for layer in range(DEPTH):
        if layer < N_A_LAYERS:
            i = layer
            x = x + gated_deltanet_mixer(rms_norm(x, a_norm_g[i]), a_w_in[i], a_conv_w[i], a_log[i], a_dt_bias[i], a_out_norm_g[i], a_w_out[i])
        else:
            if shared is None:
                shared = shared_kv(x, kv_norm_g, w_kv)
            j = layer - N_A_LAYERS
            kb, vb, k_mean = shared
            x = x + moba_mixer(rms_norm(x, b_norm_g[j]), b_w_in[j], b_w_out[j], kb, vb, k_mean, rel_bias)
    return rms_norm(x, final_norm_g)
```

```python
import functools
import math

import jax
import jax.numpy as jnp
from jax import lax
from jax.experimental import pallas as pl
from jax.experimental.pallas import tpu as pltpu

f32 = jnp.float32
bf16 = jnp.bfloat16

D_MODEL = 1024
HEADS = 8
HEAD_DIM = 128
KEY_W = HEADS * HEAD_DIM
CONV_CH = 3 * KEY_W
CONV_W = 4
GDN_IN_W = CONV_CH + KEY_W + 2 * HEADS
GDN_IN_PAD = 4224
CHUNK = 64
MOBA_BLOCK = 256
MOBA_TOPK = 3
N_BUCKETS = 32
MAX_DIST = 2048
EPS = 1e-6
NEG_INF = float("-inf")

LANES = 128
SUBLANES = 8
VMEM_LIMIT_BYTES = 56 * 1024 * 1024

MM_TM = 512
GDN_TN = 1408
MOBA_IN_TN = 1024
GDN_TB = 256
N_BIAS_TILES = MAX_DIST // MOBA_BLOCK + 2


def _cparams(sem):
    return pltpu.CompilerParams(dimension_semantics=sem, vmem_limit_bytes=VMEM_LIMIT_BYTES)


def _dot(a, b):
    return jnp.dot(a, b, preferred_element_type=f32)


def _dot_nt(a, b):
    return lax.dot_general(a, b, (((1,), (1,)), ((), ())), preferred_element_type=f32)


def _dot_tn(a, b):
    return lax.dot_general(a, b, (((0,), (0,)), ((), ())), preferred_element_type=f32)


def _split(x):
    hi = x.astype(bf16)
    lo = (x - hi.astype(f32)).astype(bf16)
    return hi, lo


def _norm_mm_kernel(x_ref, g_ref, w_ref, o_ref, xn_ref):
    @pl.when(pl.program_id(1) == 0)
    def _():
        x = x_ref[...]
        y = x * lax.rsqrt(jnp.mean(x * x, axis=-1, keepdims=True) + EPS)
        xn_ref[...] = (y * g_ref[...]).astype(bf16)

    o_ref[...] = _dot(xn_ref[...], w_ref[...]).astype(o_ref.dtype)


def _norm_matmul(x, g, w, tn, name):
    m, k = x.shape
    n = w.shape[1]
    return pl.pallas_call(
        _norm_mm_kernel,
        grid=(m // MM_TM, n // tn),
        in_specs=[
            pl.BlockSpec((MM_TM, k), lambda i, j: (i, 0)),
            pl.BlockSpec((1, k), lambda i, j: (0, 0)),
            pl.BlockSpec((k, tn), lambda i, j: (0, j)),
        ],
        out_specs=pl.BlockSpec((MM_TM, tn), lambda i, j: (i, j)),
        out_shape=jax.ShapeDtypeStruct((m, n), f32),
        scratch_shapes=[pltpu.VMEM((MM_TM, k), bf16)],
        compiler_params=_cparams(("parallel", "arbitrary")),
        name=name,
    )(x, g.reshape(1, k), w)


def _kv_kernel(x_ref, g_ref, w_ref, o_ref, km_ref, xn_ref):
    @pl.when(pl.program_id(1) == 0)
    def _():
        x = x_ref[...]
        y = x * lax.rsqrt(jnp.mean(x * x, axis=-1, keepdims=True) + EPS)
        xn_ref[...] = (y * g_ref[...]).astype(bf16)

    acc = _dot(xn_ref[...], w_ref[...])
    o_ref[...] = acc.astype(bf16)
    km_ref[0] = jnp.mean(acc, axis=0, keepdims=True)


def _kv_proj(x, g, w):
    m, k = x.shape
    n = w.shape[1]
    tn = 1024
    return pl.pallas_call(
        _kv_kernel,
        grid=(m // MOBA_BLOCK, n // tn),
        in_specs=[
            pl.BlockSpec((MOBA_BLOCK, k), lambda i, j: (i, 0)),
            pl.BlockSpec((1, k), lambda i, j: (0, 0)),
            pl.BlockSpec((k, tn), lambda i, j: (0, j)),
        ],
        out_specs=[
            pl.BlockSpec((MOBA_BLOCK, tn), lambda i, j: (i, j)),
            pl.BlockSpec((1, 1, tn), lambda i, j: (i, 0, j)),
        ],
        out_shape=[
            jax.ShapeDtypeStruct((m, n), bf16),
            jax.ShapeDtypeStruct((m // MOBA_BLOCK, 1, n), f32),
        ],
        scratch_shapes=[pltpu.VMEM((MOBA_BLOCK, k), bf16)],
        compiler_params=_cparams(("parallel", "arbitrary")),
        name="kv_proj",
    )(x, g.reshape(1, k), w)


def _out_kernel(a_ref, w_ref, r_ref, g_ref, o_ref, *, final_norm):
    y = r_ref[...] + _dot(a_ref[...], w_ref[...])
    if final_norm:
        y = y * lax.rsqrt(jnp.mean(y * y, axis=-1, keepdims=True) + EPS) * g_ref[...]
    o_ref[...] = y


def _out_proj(a, w, res, g, final_norm, name):
    m, k = a.shape
    n = w.shape[1]
    return pl.pallas_call(
        functools.partial(_out_kernel, final_norm=final_norm),
        grid=(m // MM_TM,),
        in_specs=[
            pl.BlockSpec((MM_TM, k), lambda i: (i, 0)),
            pl.BlockSpec((k, n), lambda i: (0, 0)),
            pl.BlockSpec((MM_TM, n), lambda i: (i, 0)),
            pl.BlockSpec((1, n), lambda i: (0, 0)),
        ],
        out_specs=pl.BlockSpec((MM_TM, n), lambda i: (i, 0)),
        out_shape=jax.ShapeDtypeStruct((m, n), f32),
        compiler_params=_cparams(("parallel",)),
        name=name,
    )(a, w, res, g.reshape(1, n))


def _gates_kernel(a_ref, b_ref, alog_ref, dt_ref, g_ref, beta_ref, eg_ref, er_ref):
    x = a_ref[...] + dt_ref[...]
    softplus = jnp.maximum(x, 0.0) + jnp.log1p(jnp.exp(-jnp.abs(x)))
    g = -jnp.exp(alog_ref[...]) * softplus
    pos = lax.broadcasted_iota(jnp.int32, g.shape, 1) % CHUNK
    fwd = g
    bwd = g
    s = 1
    while s < CHUNK:
        fwd = fwd + jnp.where(pos >= s, pltpu.roll(fwd, s, 1), 0.0)
        bwd = bwd + jnp.where(pos < CHUNK - s, pltpu.roll(bwd, LANES - s, 1), 0.0)
        s *= 2
    g_ref[...] = fwd
    beta_ref[...] = jax.nn.sigmoid(b_ref[...])
    eg_ref[...] = jnp.exp(fwd)
    er_ref[...] = jnp.exp(bwd - g)


def _gdn_gates(ab_t, a_log, dt_bias, bsz, seq):
    m = bsz * seq
    rows = HEADS * m // LANES
    a2 = ab_t[:HEADS].reshape(rows, LANES)
    b2 = ab_t[HEADS:].reshape(rows, LANES)
    alog = jnp.repeat(a_log.astype(f32), m // LANES).reshape(rows, 1)
    dtb = jnp.repeat(dt_bias.astype(f32), m // LANES).reshape(rows, 1)
    full = pl.BlockSpec((rows, LANES), lambda: (0, 0))
    colspec = pl.BlockSpec((rows, 1), lambda: (0, 0))
    shp = jax.ShapeDtypeStruct((rows, LANES), f32)
    g, beta, eg, er = pl.pallas_call(
        _gates_kernel,
        in_specs=[full, full, colspec, colspec],
        out_specs=[full, full, full, full],
        out_shape=[shp, shp, shp, shp],
        compiler_params=pltpu.CompilerParams(vmem_limit_bytes=VMEM_LIMIT_BYTES),
        name="gdn_gates",
    )(a2, b2, alog, dtb)
    nck = seq // CHUNK
    g4 = g.reshape(HEADS, bsz, nck, CHUNK)
    g_rows = g4.transpose(1, 2, 0, 3)
    cols = jnp.stack([t.reshape(HEADS, bsz, nck, CHUNK) for t in (g, beta, eg, er)])
    cols = cols.transpose(2, 3, 4, 0, 1).reshape(bsz, nck, CHUNK, 4 * HEADS)
    return g_rows, cols


def _neumann_inverse(a):
    n = a.shape[0]
    row = lax.broadcasted_iota(jnp.int32, (n, n), 0)
    col = lax.broadcasted_iota(jnp.int32, (n, n), 1)
    eye = jnp.where(row == col, 1.0, 0.0).astype(f32)

    def lhs_stack(x):
        hi = x.astype(bf16).astype(f32)
        return jnp.concatenate([hi, hi, x - hi], axis=1).astype(bf16)

    def rhs_stack(x):
        hi = x.astype(bf16).astype(f32)
        return jnp.concatenate([hi, x - hi, hi], axis=0).astype(bf16)

    t = eye - a
    x = a
    xr = rhs_stack(a)
    p = 1
    while True:
        x = _dot(lhs_stack(x), xr)
        xr = rhs_stack(x)
        p *= 2
        t = t + _dot(lhs_stack(t), xr)
        if 2 * p >= n:
            return t


def _gdn_kernel(x_ref, z_ref, cw_ref, grow_ref, cols_ref, og_ref, o_ref,
                stage, qkvn, state):
    tb = x_ref.shape[0]
    t = pl.program_id(1)

    @pl.when(t == 0)
    def _():
        stage[0:SUBLANES, :] = jnp.zeros((SUBLANES, CONV_CH), f32)
        state[...] = jnp.zeros_like(state)

    stage[SUBLANES:SUBLANES + tb, :] = x_ref[...]
    for cb in range(CONV_CH // LANES):
        cs = slice(cb * LANES, (cb + 1) * LANES)
        acc = stage[SUBLANES:SUBLANES + tb, cs] * cw_ref[CONV_W - 1:CONV_W, cs]
        for j in range(CONV_W - 1):
            lo = SUBLANES - (CONV_W - 1) + j
            acc = acc + stage[lo:lo + tb, cs] * cw_ref[j:j + 1, cs]
        y = acc * jax.nn.sigmoid(acc)
        if cb < 2 * HEADS:
            y = y * lax.rsqrt(jnp.sum(y * y, axis=-1, keepdims=True) + EPS)
            if cb < HEADS:
                y = y * (HEAD_DIM ** -0.5)
        qkvn[:, cs] = y
    stage[0:SUBLANES, :] = stage[tb:tb + SUBLANES, :]

    row = lax.broadcasted_iota(jnp.int32, (CHUNK, CHUNK), 0)
    col = lax.broadcasted_iota(jnp.int32, (CHUNK, CHUNK), 1)
    tril = row >= col
    strict = row > col
    og = og_ref[...]

    def chunk_step(c, carry):
        r0 = pl.multiple_of(c * CHUNK, CHUNK)
        rows = pl.ds(r0, CHUNK)
        g_rows = grow_ref[0, c]
        ctile = cols_ref[0, c]
        for h in range(HEADS):
            q = qkvn[rows, h * HEAD_DIM:(h + 1) * HEAD_DIM]
            k = qkvn[rows, KEY_W + h * HEAD_DIM:KEY_W + (h + 1) * HEAD_DIM]
            v = qkvn[rows, 2 * KEY_W + h * HEAD_DIM:2 * KEY_W + (h + 1) * HEAD_DIM]
            g_row = g_rows[h:h + 1, :]
            g_col = ctile[:, h:h + 1]
            beta = ctile[:, HEADS + h:HEADS + h + 1]
            eg = ctile[:, 2 * HEADS + h:2 * HEADS + h + 1]
            er = ctile[:, 3 * HEADS + h:3 * HEADS + h + 1]
            g_last = eg[CHUNK - 1:CHUNK, :]

            decay = jnp.exp(jnp.where(tril, g_col - g_row, NEG_INF))
            k_beta = k * beta
            kq = _dot_nt(jnp.concatenate([k_beta, q], axis=0).astype(bf16), k.astype(bf16))
            a_mat = jnp.where(strict, kq[:CHUNK] * decay, 0.0)
            qk = kq[CHUNK:] * decay
            t_inv = _neumann_inverse(a_mat)
            rhs = jnp.concatenate([v * beta, k_beta * eg], axis=1).astype(bf16)
            uw = _dot(t_inv.astype(bf16), rhs)
            u = uw[:, :HEAD_DIM]
            w = uw[:, HEAD_DIM:]

            s_prev = state[h]
            s_bf = s_prev.astype(bf16)
            ws_qs = _dot(jnp.concatenate([w, q * eg], axis=0).astype(bf16), s_bf)
            v_new = u - ws_qs[:CHUNK]
            v_new_bf = v_new.astype(bf16)
            o = ws_qs[CHUNK:] + _dot(qk.astype(bf16), v_new_bf)
            state[h] = s_prev * g_last + _dot_tn((k * er).astype(bf16), v_new_bf)

            o = o * lax.rsqrt(jnp.mean(o * o, axis=-1, keepdims=True) + EPS) * og
            z = z_ref[rows, h * HEAD_DIM:(h + 1) * HEAD_DIM]
            o_ref[rows, h * HEAD_DIM:(h + 1) * HEAD_DIM] = (o * (z * jax.nn.sigmoid(z))).astype(bf16)
        return carry

    lax.fori_loop(0, tb // CHUNK, chunk_step, 0)


def _gdn_core(proj, conv_w, g_rows, cols, out_norm_g, bsz, seq, name):
    m = bsz * seq
    nt = seq // GDN_TB
    cpt = GDN_TB // CHUNK
    return pl.pallas_call(
        _gdn_kernel,
        grid=(bsz, nt),
        in_specs=[
            pl.BlockSpec((GDN_TB, CONV_CH), lambda b, t: (b * nt + t, 0)),
            pl.BlockSpec((GDN_TB, KEY_W), lambda b, t: (b * nt + t, CONV_CH // KEY_W)),
            pl.BlockSpec((CONV_W, CONV_CH), lambda b, t: (0, 0)),
            pl.BlockSpec((1, cpt, HEADS, CHUNK), lambda b, t: (b, t, 0, 0)),
            pl.BlockSpec((1, cpt, CHUNK, 4 * HEADS), lambda b, t: (b, t, 0, 0)),
            pl.BlockSpec((1, HEAD_DIM), lambda b, t: (0, 0)),
        ],
        out_specs=pl.BlockSpec((GDN_TB, KEY_W), lambda b, t: (b * nt + t, 0)),
        out_shape=jax.ShapeDtypeStruct((m, KEY_W), bf16),
        scratch_shapes=[
            pltpu.VMEM((GDN_TB + SUBLANES, CONV_CH), f32),
            pltpu.VMEM((GDN_TB, CONV_CH), f32),
            pltpu.VMEM((HEADS, HEAD_DIM, HEAD_DIM), f32),
        ],
        compiler_params=_cparams(("parallel", "arbitrary")),
        name=name,
    )(proj, proj, conv_w, g_rows, cols, out_norm_g.reshape(1, HEAD_DIM))


def _bias_kernel(rb_ref, o_ref):
    h = pl.program_id(0)
    d = pl.program_id(1)
    i = lax.broadcasted_iota(jnp.int32, (MOBA_BLOCK, MOBA_BLOCK), 0)
    j = lax.broadcasted_iota(jnp.int32, (MOBA_BLOCK, MOBA_BLOCK), 1)
    n = jnp.maximum(d * MOBA_BLOCK + i - j, 0)
    max_exact = N_BUCKETS // 2
    nf = jnp.maximum(n, 1).astype(f32)
    large = max_exact + (jnp.log(nf / max_exact) / math.log(MAX_DIST / max_exact)
                         * (N_BUCKETS - max_exact)).astype(jnp.int32)
    large = jnp.minimum(large, N_BUCKETS - 1)
    bucket = jnp.where(n < max_exact, n, large)
    out = jnp.zeros((MOBA_BLOCK, MOBA_BLOCK), f32)
    for b in range(N_BUCKETS):
        out = jnp.where(bucket == b, rb_ref[h, b], out)
    o_ref[0, 0] = out


def _bias_tiles(rel_bias):
    return pl.pallas_call(
        _bias_kernel,
        grid=(HEADS, N_BIAS_TILES),
        in_specs=[pl.BlockSpec(memory_space=pltpu.SMEM)],
        out_specs=pl.BlockSpec((1, 1, MOBA_BLOCK, MOBA_BLOCK), lambda h, d: (h, d, 0, 0)),
        out_shape=jax.ShapeDtypeStruct((HEADS, N_BIAS_TILES, MOBA_BLOCK, MOBA_BLOCK), f32),
        compiler_params=_cparams(("parallel", "parallel")),
        name="t5_bias_tiles",
    )(rel_bias.T.astype(f32))


def _moba_kernel(q_ref, z_ref, k_ref, v_ref, km_ref, bias_ref, o_ref):
    cur = pl.program_id(2)
    nblk = km_ref.shape[1]
    q = q_ref[...]
    scale = HEAD_DIM ** -0.5

    qh, ql = _split(q)
    kmh, kml = _split(km_ref[0])
    gate = _dot_nt(qh, kmh) + _dot_nt(qh, kml) + _dot_nt(ql, kmh)
    blk = lax.broadcasted_iota(jnp.int32, (MOBA_BLOCK, nblk), 1).astype(f32)
    gate = jnp.where(blk < cur.astype(f32), gate, NEG_INF)
    sels = []
    for _ in range(MOBA_TOPK):
        best = jnp.max(gate, axis=1, keepdims=True)
        idx = jnp.min(jnp.where(gate == best, blk, float(nblk)), axis=1, keepdims=True)
        idx = jnp.where(best > NEG_INF, idx, -1.0)
        sels.append(idx)
        gate = jnp.where(blk == idx, NEG_INF, gate)

    qb = q.astype(bf16)

    def block_scores(j, d):
        kj = k_ref[pl.ds(pl.multiple_of(j * MOBA_BLOCK, MOBA_BLOCK), MOBA_BLOCK), :]
        return _dot_nt(qb, kj) * scale + bias_ref[0, d]

    def block_values(j):
        return v_ref[pl.ds(pl.multiple_of(j * MOBA_BLOCK, MOBA_BLOCK), MOBA_BLOCK), :]

    row = lax.broadcasted_iota(jnp.int32, (MOBA_BLOCK, MOBA_BLOCK), 0)
    col = lax.broadcasted_iota(jnp.int32, (MOBA_BLOCK, MOBA_BLOCK), 1)
    s = jnp.where(row >= col, block_scores(cur, 0), NEG_INF)
    m0 = jnp.max(s, axis=1, keepdims=True)
    p = jnp.exp(s - m0)
    l0 = jnp.sum(p, axis=1, keepdims=True)
    acc0 = _dot(p.astype(bf16), block_values(cur))

    def body(j, carry):
        m_prev, l_prev, acc = carry
        jf = j.astype(f32)
        chosen = (sels[0] == jf) | (sels[1] == jf) | (sels[2] == jf)
        s = block_scores(j, jnp.minimum(cur - j, N_BIAS_TILES - 1))
        s = jnp.where(chosen, s, NEG_INF)
        m_new = jnp.maximum(m_prev, jnp.max(s, axis=1, keepdims=True))
        alpha = jnp.exp(m_prev - m_new)
        p = jnp.exp(s - m_new)
        l_new = alpha * l_prev + jnp.sum(p, axis=1, keepdims=True)
        acc = alpha * acc + _dot(p.astype(bf16), block_values(j))
        return m_new, l_new, acc

    _, l_fin, acc = lax.fori_loop(0, cur, body, (m0, l0, acc0))
    z = z_ref[...]
    o_ref[...] = ((acc / l_fin) * (z * jax.nn.sigmoid(z))).astype(bf16)


def _moba_attention(qz, kv, k_mean, bias, bsz, seq, name):
    m = bsz * seq
    nq = seq // MOBA_BLOCK
    return pl.pallas_call(
        _moba_kernel,
        grid=(bsz, HEADS, nq),
        in_specs=[
            pl.BlockSpec((MOBA_BLOCK, HEAD_DIM), lambda b, h, i: (b * nq + i, h)),
            pl.BlockSpec((MOBA_BLOCK, HEAD_DIM), lambda b, h, i: (b * nq + i, HEADS + h)),
            pl.BlockSpec((seq, HEAD_DIM), lambda b, h, i: (b, h)),
            pl.BlockSpec((seq, HEAD_DIM), lambda b, h, i: (b, HEADS + h)),
            pl.BlockSpec((1, nq, HEAD_DIM), lambda b, h, i: (b, 0, h)),
            pl.BlockSpec((1, N_BIAS_TILES, MOBA_BLOCK, MOBA_BLOCK), lambda b, h, i: (h, 0, 0, 0)),
        ],
        out_specs=pl.BlockSpec((MOBA_BLOCK, HEAD_DIM), lambda b, h, i: (b * nq + i, h)),
        out_shape=jax.ShapeDtypeStruct((m, KEY_W), bf16),
        compiler_params=_cparams(("parallel", "parallel", "arbitrary")),
        name=name,
    )(qz, qz, kv, kv, k_mean, bias)


def kernel(x, a_norm_g, a_w_in, a_conv_w, a_log, a_dt_bias, a_out_norm_g, a_w_out, kv_norm_g, w_kv, b_norm_g, b_w_in, b_w_out, rel_bias, final_norm_g):
    bsz, seq, d = x.shape
    m = bsz * seq
    assert d == D_MODEL and seq % MOBA_BLOCK == 0 and seq % GDN_TB == 0 and m % MM_TM == 0
    xf = x.reshape(m, d).astype(f32)

    for i in range(a_w_in.shape[0]):
        w_in = jnp.pad(a_w_in[i], ((0, 0), (0, GDN_IN_PAD - GDN_IN_W))).astype(bf16)
        proj = _norm_matmul(xf, a_norm_g[i], w_in, GDN_TN, f"gdn{i}_in_proj")
        ab_t = proj[:, CONV_CH + KEY_W:GDN_IN_W].T
        g_rows, cols = _gdn_gates(ab_t, a_log[i], a_dt_bias[i], bsz, seq)
        o = _gdn_core(proj, a_conv_w[i].astype(f32), g_rows, cols, a_out_norm_g[i].astype(f32),
                      bsz, seq, f"gdn{i}_core")
        xf = _out_proj(o, a_w_out[i].astype(bf16), xf, final_norm_g, False, f"gdn{i}_out_proj")

    kv, k_mean = _kv_proj(xf, kv_norm_g, w_kv.astype(bf16))
    k_mean = k_mean.reshape(bsz, seq // MOBA_BLOCK, 2 * KEY_W)
    bias = _bias_tiles(rel_bias)

    n_b = b_w_in.shape[0]
    for j in range(n_b):
        qz = _norm_matmul(xf, b_norm_g[j], b_w_in[j].astype(bf16), MOBA_IN_TN, f"moba{j}_in_proj")
        o = _moba_attention(qz, kv, k_mean, bias, bsz, seq, f"moba{j}_attn")
        xf = _out_proj(o, b_w_out[j].astype(bf16), xf, final_norm_g, j == n_b - 1, f"moba{j}_out_proj")
    return xf.reshape(bsz, seq, d).astype(x.dtype)
```

```python
import functools
import math

import jax
import jax.numpy as jnp
from jax import lax
from jax.experimental import pallas as pl
from jax.experimental.pallas import tpu as pltpu

f32 = jnp.float32
bf16 = jnp.bfloat16

D_MODEL = 1024
HEADS = 8
HEAD_DIM = 128
KEY_W = HEADS * HEAD_DIM
CONV_CH = 3 * KEY_W
CONV_W = 4
GDN_IN_W = CONV_CH + KEY_W + 2 * HEADS
GDN_IN_PAD = 4224
CHUNK = 64
MOBA_BLOCK = 256
MOBA_TOPK = 3
N_BUCKETS = 32
MAX_DIST = 2048
EPS = 1e-6
NEG_INF = float("-inf")
LOG2E = math.log2(math.e)

LANES = 128
SUBLANES = 8
VMEM_LIMIT_BYTES = 56 * 1024 * 1024

MM_TM = 512
GDN_TN = 1408
MOBA_IN_TN = 1024
GDN_TB = 256
MOBA_UNROLL = 4
MOBA_CHAINS = 2
N_BIAS_TILES = MAX_DIST // MOBA_BLOCK + 2


def _cparams(sem):
    return pltpu.CompilerParams(dimension_semantics=sem, vmem_limit_bytes=VMEM_LIMIT_BYTES)


def _dot(a, b):
    return jnp.dot(a, b, preferred_element_type=f32)


def _dot_nt(a, b):
    return lax.dot_general(a, b, (((1,), (1,)), ((), ())), preferred_element_type=f32)


def _dot_tn(a, b):
    return lax.dot_general(a, b, (((0,), (0,)), ((), ())), preferred_element_type=f32)


def _split(x):
    hi = x.astype(bf16)
    lo = (x - hi.astype(f32)).astype(bf16)
    return hi, lo


def _norm_mm_kernel(x_ref, g_ref, w_ref, o_ref, xn_ref):
    @pl.when(pl.program_id(1) == 0)
    def _():
        x = x_ref[...]
        y = x * lax.rsqrt(jnp.mean(x * x, axis=-1, keepdims=True) + EPS)
        xn_ref[...] = (y * g_ref[...]).astype(bf16)

    o_ref[...] = _dot(xn_ref[...], w_ref[...]).astype(o_ref.dtype)


def _norm_matmul(x, g, w, tn, name):
    m, k = x.shape
    n = w.shape[1]
    return pl.pallas_call(
        _norm_mm_kernel,
        grid=(m // MM_TM, n // tn),
        in_specs=[
            pl.BlockSpec((MM_TM, k), lambda i, j: (i, 0)),
            pl.BlockSpec((1, k), lambda i, j: (0, 0)),
            pl.BlockSpec((k, tn), lambda i, j: (0, j)),
        ],
        out_specs=pl.BlockSpec((MM_TM, tn), lambda i, j: (i, j)),
        out_shape=jax.ShapeDtypeStruct((m, n), f32),
        scratch_shapes=[pltpu.VMEM((MM_TM, k), bf16)],
        compiler_params=_cparams(("parallel", "arbitrary")),
        name=name,
    )(x, g.reshape(1, k), w)


def _kv_kernel(x_ref, g_ref, w_ref, k_ref, vt_ref, km_ref):
    x = x_ref[...]
    y = x * lax.rsqrt(jnp.mean(x * x, axis=-1, keepdims=True) + EPS)
    acc = _dot((y * g_ref[...]).astype(bf16), w_ref[...])
    k = acc[:, :KEY_W]
    k_ref[...] = k.astype(bf16)
    km_ref[0] = jnp.mean(k, axis=0, keepdims=True)
    vt_ref[0] = acc[:, KEY_W:].T.astype(bf16)


def _kv_proj(x, g, w, bsz, seq):
    m, k = x.shape
    nblk = seq // MOBA_BLOCK
    return pl.pallas_call(
        _kv_kernel,
        grid=(m // MOBA_BLOCK,),
        in_specs=[
            pl.BlockSpec((MOBA_BLOCK, k), lambda i: (i, 0)),
            pl.BlockSpec((1, k), lambda i: (0, 0)),
            pl.BlockSpec((k, 2 * KEY_W), lambda i: (0, 0)),
        ],
        out_specs=[
            pl.BlockSpec((MOBA_BLOCK, KEY_W), lambda i: (i, 0)),
            pl.BlockSpec((1, KEY_W, MOBA_BLOCK), lambda i: (i // nblk, 0, i % nblk)),
            pl.BlockSpec((1, 1, KEY_W), lambda i: (i, 0, 0)),
        ],
        out_shape=[
            jax.ShapeDtypeStruct((m, KEY_W), bf16),
            jax.ShapeDtypeStruct((bsz, KEY_W, seq), bf16),
            jax.ShapeDtypeStruct((m // MOBA_BLOCK, 1, KEY_W), f32),
        ],
        compiler_params=_cparams(("parallel",)),
        name="kv_proj",
    )(x, g.reshape(1, k), w)


def _out_kernel(a_ref, w_ref, r_ref, g_ref, o_ref, *, final_norm):
    y = r_ref[...] + _dot(a_ref[...], w_ref[...])
    if final_norm:
        y = y * lax.rsqrt(jnp.mean(y * y, axis=-1, keepdims=True) + EPS) * g_ref[...]
    o_ref[...] = y


def _out_proj(a, w, res, g, final_norm, name):
    m, k = a.shape
    n = w.shape[1]
    return pl.pallas_call(
        functools.partial(_out_kernel, final_norm=final_norm),
        grid=(m // MM_TM,),
        in_specs=[
            pl.BlockSpec((MM_TM, k), lambda i: (i, 0)),
            pl.BlockSpec((k, n), lambda i: (0, 0)),
            pl.BlockSpec((MM_TM, n), lambda i: (i, 0)),
            pl.BlockSpec((1, n), lambda i: (0, 0)),
        ],
        out_specs=pl.BlockSpec((MM_TM, n), lambda i: (i, 0)),
        out_shape=jax.ShapeDtypeStruct((m, n), f32),
        compiler_params=_cparams(("parallel",)),
        name=name,
    )(a, w, res, g.reshape(1, n))


def _gates_kernel(a_ref, b_ref, alog_ref, dt_ref, g_ref, beta_ref, eg_ref, er_ref):
    x = a_ref[...] + dt_ref[...]
    softplus = jnp.maximum(x, 0.0) + jnp.log1p(jnp.exp(-jnp.abs(x)))
    g = -jnp.exp(alog_ref[...]) * softplus
    pos = lax.broadcasted_iota(jnp.int32, g.shape, 1) % CHUNK
    fwd = g
    bwd = g
    s = 1
    while s < CHUNK:
        fwd = fwd + jnp.where(pos >= s, pltpu.roll(fwd, s, 1), 0.0)
        bwd = bwd + jnp.where(pos < CHUNK - s, pltpu.roll(bwd, LANES - s, 1), 0.0)
        s *= 2
    g_ref[...] = fwd
    beta_ref[...] = jax.nn.sigmoid(b_ref[...])
    eg_ref[...] = jnp.exp(fwd)
    er_ref[...] = jnp.exp(bwd - g)


def _gdn_gates(ab_t, a_log, dt_bias, bsz, seq):
    m = bsz * seq
    rows = HEADS * m // LANES
    a2 = ab_t[:HEADS].reshape(rows, LANES)
    b2 = ab_t[HEADS:].reshape(rows, LANES)
    alog = jnp.repeat(a_log.astype(f32), m // LANES).reshape(rows, 1)
    dtb = jnp.repeat(dt_bias.astype(f32), m // LANES).reshape(rows, 1)
    full = pl.BlockSpec((rows, LANES), lambda: (0, 0))
    colspec = pl.BlockSpec((rows, 1), lambda: (0, 0))
    shp = jax.ShapeDtypeStruct((rows, LANES), f32)
    g, beta, eg, er = pl.pallas_call(
        _gates_kernel,
        in_specs=[full, full, colspec, colspec],
        out_specs=[full, full, full, full],
        out_shape=[shp, shp, shp, shp],
        compiler_params=pltpu.CompilerParams(vmem_limit_bytes=VMEM_LIMIT_BYTES),
        name="gdn_gates",
    )(a2, b2, alog, dtb)
    nck = seq // CHUNK
    g4 = g.reshape(HEADS, bsz, nck, CHUNK)
    g_rows = g4.transpose(1, 2, 0, 3)
    cols = jnp.stack([t.reshape(HEADS, bsz, nck, CHUNK) for t in (g, beta, eg, er)])
    cols = cols.transpose(2, 3, 4, 0, 1).reshape(bsz, nck, CHUNK, 4 * HEADS)
    return g_rows, cols


def _neumann_inverse(mats):
    n = mats[0].shape[0]
    row = lax.broadcasted_iota(jnp.int32, (n, n), 0)
    col = lax.broadcasted_iota(jnp.int32, (n, n), 1)
    eye = jnp.where(row == col, 1.0, 0.0).astype(f32)

    def lhs_stack(x):
        hi = x.astype(bf16).astype(f32)
        return jnp.concatenate([hi, hi, x - hi], axis=1).astype(bf16)

    def rhs_stack(x):
        hi = x.astype(bf16).astype(f32)
        return jnp.concatenate([hi, x - hi, hi], axis=0).astype(bf16)

    ts = [eye - a for a in mats]
    xs = list(mats)
    xrs = [rhs_stack(a) for a in mats]
    p = 1
    while True:
        xs = [_dot(lhs_stack(x), xr) for x, xr in zip(xs, xrs)]
        xrs = [rhs_stack(x) for x in xs]
        p *= 2
        ts = [t + _dot(lhs_stack(t), xr) for t, xr in zip(ts, xrs)]
        if 2 * p >= n:
            return ts


def _gdn_kernel(x_ref, z_ref, cw_ref, grow_ref, cols_ref, og_ref, o_ref,
                stage, qkvn, state):
    tb = x_ref.shape[0]
    t = pl.program_id(1)

    @pl.when(t == 0)
    def _():
        stage[0:SUBLANES, :] = jnp.zeros((SUBLANES, CONV_CH), f32)
        state[...] = jnp.zeros_like(state)

    stage[SUBLANES:SUBLANES + tb, :] = x_ref[...]
    for cb in range(CONV_CH // LANES):
        cs = slice(cb * LANES, (cb + 1) * LANES)
        acc = stage[SUBLANES:SUBLANES + tb, cs] * cw_ref[CONV_W - 1:CONV_W, cs]
        for j in range(CONV_W - 1):
            lo = SUBLANES - (CONV_W - 1) + j
            acc = acc + stage[lo:lo + tb, cs] * cw_ref[j:j + 1, cs]
        y = acc * jax.nn.sigmoid(acc)
        if cb < 2 * HEADS:
            y = y * lax.rsqrt(jnp.sum(y * y, axis=-1, keepdims=True) + EPS)
            if cb < HEADS:
                y = y * (HEAD_DIM ** -0.5)
        qkvn[:, cs] = y
    stage[0:SUBLANES, :] = stage[tb:tb + SUBLANES, :]

    row = lax.broadcasted_iota(jnp.int32, (CHUNK, CHUNK), 0)
    col = lax.broadcasted_iota(jnp.int32, (CHUNK, CHUNK), 1)
    tril = row >= col
    strict = row > col
    og = og_ref[...]

    def chunk_step(c, carry):
        r0 = pl.multiple_of(c * CHUNK, CHUNK)
        rows = pl.ds(r0, CHUNK)
        g_rows = grow_ref[0, c]
        ctile = cols_ref[0, c]
        hs = range(HEADS)

        def head_cols(base):
            return [slice(base + h * HEAD_DIM, base + (h + 1) * HEAD_DIM) for h in hs]

        q = [qkvn[rows, cs] for cs in head_cols(0)]
        k = [qkvn[rows, cs] for cs in head_cols(KEY_W)]
        v = [qkvn[rows, cs] for cs in head_cols(2 * KEY_W)]
        beta = [ctile[:, HEADS + h:HEADS + h + 1] for h in hs]
        eg = [ctile[:, 2 * HEADS + h:2 * HEADS + h + 1] for h in hs]
        er = [ctile[:, 3 * HEADS + h:3 * HEADS + h + 1] for h in hs]
        decay = [jnp.exp(jnp.where(tril, ctile[:, h:h + 1] - g_rows[h:h + 1, :], NEG_INF)) for h in hs]
        k_beta = [k[h] * beta[h] for h in hs]
        kq = [_dot_nt(jnp.concatenate([k_beta[h], q[h]], axis=0).astype(bf16), k[h].astype(bf16)) for h in hs]
        t_inv = _neumann_inverse([jnp.where(strict, kq[h][:CHUNK] * decay[h], 0.0) for h in hs])
        uw = [_dot(t_inv[h].astype(bf16),
                   jnp.concatenate([v[h] * beta[h], k_beta[h] * eg[h]], axis=1).astype(bf16)) for h in hs]
        s_prev = [state[h] for h in hs]
        ws_qs = [_dot(jnp.concatenate([uw[h][:, HEAD_DIM:], q[h] * eg[h]], axis=0).astype(bf16),
                      s_prev[h].astype(bf16)) for h in hs]
        v_new = [(uw[h][:, :HEAD_DIM] - ws_qs[h][:CHUNK]).astype(bf16) for h in hs]
        o = [ws_qs[h][CHUNK:] + _dot((kq[h][CHUNK:] * decay[h]).astype(bf16), v_new[h]) for h in hs]
        for h in hs:
            g_last = eg[h][CHUNK - 1:CHUNK, :]
            state[h] = s_prev[h] * g_last + _dot_tn((k[h] * er[h]).astype(bf16), v_new[h])
        for h, cs in zip(hs, head_cols(0)):
            on = o[h] * lax.rsqrt(jnp.mean(o[h] * o[h], axis=-1, keepdims=True) + EPS) * og
            z = z_ref[rows, cs]
            o_ref[rows, cs] = (on * (z * jax.nn.sigmoid(z))).astype(bf16)
        return carry

    lax.fori_loop(0, tb // CHUNK, chunk_step, 0)


def _gdn_core(proj, conv_w, g_rows, cols, out_norm_g, bsz, seq, name):
    m = bsz * seq
    nt = seq // GDN_TB
    cpt = GDN_TB // CHUNK
    return pl.pallas_call(
        _gdn_kernel,
        grid=(bsz, nt),
        in_specs=[
            pl.BlockSpec((GDN_TB, CONV_CH), lambda b, t: (b * nt + t, 0)),
            pl.BlockSpec((GDN_TB, KEY_W), lambda b, t: (b * nt + t, CONV_CH // KEY_W)),
            pl.BlockSpec((CONV_W, CONV_CH), lambda b, t: (0, 0)),
            pl.BlockSpec((1, cpt, HEADS, CHUNK), lambda b, t: (b, t, 0, 0)),
            pl.BlockSpec((1, cpt, CHUNK, 4 * HEADS), lambda b, t: (b, t, 0, 0)),
            pl.BlockSpec((1, HEAD_DIM), lambda b, t: (0, 0)),
        ],
        out_specs=pl.BlockSpec((GDN_TB, KEY_W), lambda b, t: (b * nt + t, 0)),
        out_shape=jax.ShapeDtypeStruct((m, KEY_W), bf16),
        scratch_shapes=[
            pltpu.VMEM((GDN_TB + SUBLANES, CONV_CH), f32),
            pltpu.VMEM((GDN_TB, CONV_CH), f32),
            pltpu.VMEM((HEADS, HEAD_DIM, HEAD_DIM), f32),
        ],
        compiler_params=_cparams(("parallel", "arbitrary")),
        name=name,
    )(proj, proj, conv_w, g_rows, cols, out_norm_g.reshape(1, HEAD_DIM))


def _bias_kernel(rb_ref, o_ref):
    h = pl.program_id(0)
    d = pl.program_id(1)
    key = lax.broadcasted_iota(jnp.int32, (MOBA_BLOCK, MOBA_BLOCK), 0)
    qry = lax.broadcasted_iota(jnp.int32, (MOBA_BLOCK, MOBA_BLOCK), 1)
    n = jnp.maximum(d * MOBA_BLOCK + qry - key, 0)
    max_exact = N_BUCKETS // 2
    nf = jnp.maximum(n, 1).astype(f32)
    large = max_exact + (jnp.log(nf / max_exact) / math.log(MAX_DIST / max_exact)
                         * (N_BUCKETS - max_exact)).astype(jnp.int32)
    large = jnp.minimum(large, N_BUCKETS - 1)
    bucket = jnp.where(n < max_exact, n, large)
    out = jnp.zeros((MOBA_BLOCK, MOBA_BLOCK), f32)
    for b in range(N_BUCKETS):
        out = jnp.where(bucket == b, rb_ref[h, b], out)
    o_ref[0, 0] = out * LOG2E


def _bias_tiles(rel_bias):
    return pl.pallas_call(
        _bias_kernel,
        grid=(HEADS, N_BIAS_TILES),
        in_specs=[pl.BlockSpec(memory_space=pltpu.SMEM)],
        out_specs=pl.BlockSpec((1, 1, MOBA_BLOCK, MOBA_BLOCK), lambda h, d: (h, d, 0, 0)),
        out_shape=jax.ShapeDtypeStruct((HEADS, N_BIAS_TILES, MOBA_BLOCK, MOBA_BLOCK), f32),
        compiler_params=_cparams(("parallel", "parallel")),
        name="t5_bias_tiles",
    )(rel_bias.T.astype(f32))


def _moba_kernel(q_ref, z_ref, k_ref, vt_ref, km_ref, bias_ref, o_ref, s_scr):
    cur = pl.program_id(2)
    nblk = km_ref.shape[1]
    q_t = q_ref[...].T

    qh, ql = _split(q_t)
    kmh, kml = _split(km_ref[0])
    gate = _dot(kmh, qh) + _dot(kml, qh) + _dot(kmh, ql)
    blk = lax.broadcasted_iota(jnp.int32, (nblk, MOBA_BLOCK), 0).astype(f32)
    gate = jnp.where(blk < cur.astype(f32), gate, NEG_INF)
    sels = []
    for _ in range(MOBA_TOPK):
        best = jnp.max(gate, axis=0, keepdims=True)
        idx = jnp.min(jnp.where(gate == best, blk, float(nblk)), axis=0, keepdims=True)
        idx = jnp.where(best > NEG_INF, idx, -1.0)
        sels.append(idx)
        gate = jnp.where(blk == idx, NEG_INF, gate)

    qb = (q_t * (HEAD_DIM ** -0.5 * LOG2E)).astype(bf16)

    def block_scores(j):
        jc = jnp.minimum(j, nblk - 1)
        dist = jnp.clip(cur - jc, 0, N_BIAS_TILES - 1)
        kj = k_ref[pl.ds(pl.multiple_of(jc * MOBA_BLOCK, MOBA_BLOCK), MOBA_BLOCK), :]
        return _dot(kj, qb) + bias_ref[0, dist]

    def block_values(j):
        jc = jnp.minimum(j, nblk - 1)
        return vt_ref[0, :, pl.ds(pl.multiple_of(jc * MOBA_BLOCK, MOBA_BLOCK), MOBA_BLOCK)]

    key = lax.broadcasted_iota(jnp.int32, (MOBA_BLOCK, MOBA_BLOCK), 0)
    qry = lax.broadcasted_iota(jnp.int32, (MOBA_BLOCK, MOBA_BLOCK), 1)
    s = jnp.where(qry >= key, block_scores(cur), NEG_INF)
    m0 = jnp.max(s, axis=0, keepdims=True)
    p = jnp.exp2(s - m0)
    l0 = jnp.sum(p, axis=0, keepdims=True)
    acc0 = _dot(block_values(cur), p.astype(bf16))

    for u in range(MOBA_UNROLL):
        s_scr[u] = block_scores(jnp.int32(u))
    per_chain = MOBA_UNROLL // MOBA_CHAINS

    def step(it, chains):
        base = it * MOBA_UNROLL
        s_cur = [s_scr[u] for u in range(MOBA_UNROLL)]
        s_next = [block_scores(base + MOBA_UNROLL + u) for u in range(MOBA_UNROLL)]
        out = []
        for c, (m_prev, l_prev, acc) in enumerate(chains):
            us = range(c * per_chain, (c + 1) * per_chain)
            chosen = []
            for u in us:
                j = base + u
                jf = j.astype(f32)
                chosen.append(((sels[0] == jf) | (sels[1] == jf) | (sels[2] == jf)) & (j < cur))
            m_new = m_prev
            for u, ch in zip(us, chosen):
                m_new = jnp.maximum(m_new, jnp.where(ch, jnp.max(s_cur[u], axis=0, keepdims=True), NEG_INF))
            alpha = jnp.exp2(m_prev - m_new)
            l_new = alpha * l_prev
            acc = alpha * acc
            for u, ch in zip(us, chosen):
                p = jnp.exp2(s_cur[u] - jnp.where(ch, m_new, float("inf")))
                l_new = l_new + jnp.sum(p, axis=0, keepdims=True)
                acc = acc + _dot(block_values(base + u), p.astype(bf16))
            out.append((m_new, l_new, acc))
        for u in range(MOBA_UNROLL):
            s_scr[u] = s_next[u]
        return tuple(out)

    init = ((m0, l0, acc0),) + ((m0, jnp.zeros_like(l0), jnp.zeros_like(acc0)),) * (MOBA_CHAINS - 1)
    chains = lax.fori_loop(0, (cur + MOBA_UNROLL - 1) // MOBA_UNROLL, step, init)
    m_fin, l_fin, acc = chains[0]
    for m_c, l_c, acc_c in chains[1:]:
        m_new = jnp.maximum(m_fin, m_c)
        a_old = jnp.exp2(m_fin - m_new)
        a_c = jnp.exp2(m_c - m_new)
        l_fin = a_old * l_fin + a_c * l_c
        acc = a_old * acc + a_c * acc_c
        m_fin = m_new
    z = z_ref[...]
    o_ref[...] = ((acc / l_fin).T * (z * jax.nn.sigmoid(z))).astype(bf16)


def _moba_attention(qz, k, v_t, k_mean, bias, bsz, seq, name):
    m = bsz * seq
    nq = seq // MOBA_BLOCK
    return pl.pallas_call(
        _moba_kernel,
        grid=(bsz, HEADS, nq),
        in_specs=[
            pl.BlockSpec((MOBA_BLOCK, HEAD_DIM), lambda b, h, i: (b * nq + i, h)),
            pl.BlockSpec((MOBA_BLOCK, HEAD_DIM), lambda b, h, i: (b * nq + i, HEADS + h)),
            pl.BlockSpec((seq, HEAD_DIM), lambda b, h, i: (b, h)),
            pl.BlockSpec((1, HEAD_DIM, seq), lambda b, h, i: (b, h, 0)),
            pl.BlockSpec((1, nq, HEAD_DIM), lambda b, h, i: (b, 0, h)),
            pl.BlockSpec((1, N_BIAS_TILES, MOBA_BLOCK, MOBA_BLOCK), lambda b, h, i: (h, 0, 0, 0)),
        ],
        out_specs=pl.BlockSpec((MOBA_BLOCK, HEAD_DIM), lambda b, h, i: (b * nq + i, h)),
        out_shape=jax.ShapeDtypeStruct((m, KEY_W), bf16),
        scratch_shapes=[pltpu.VMEM((MOBA_UNROLL, MOBA_BLOCK, MOBA_BLOCK), f32)],
        compiler_params=_cparams(("parallel", "parallel", "arbitrary")),
        name=name,
    )(qz, qz, k, v_t, k_mean, bias)


def kernel(x, a_norm_g, a_w_in, a_conv_w, a_log, a_dt_bias, a_out_norm_g, a_w_out, kv_norm_g, w_kv, b_norm_g, b_w_in, b_w_out, rel_bias, final_norm_g):
    bsz, seq, d = x.shape
    m = bsz * seq
    assert d == D_MODEL and seq % MOBA_BLOCK == 0 and seq % GDN_TB == 0 and m % MM_TM == 0
    xf = x.reshape(m, d).astype(f32)

    for i in range(a_w_in.shape[0]):
        w_in = jnp.pad(a_w_in[i], ((0, 0), (0, GDN_IN_PAD - GDN_IN_W))).astype(bf16)
        proj = _norm_matmul(xf, a_norm_g[i], w_in, GDN_TN, f"gdn{i}_in_proj")
        ab_t = proj[:, CONV_CH + KEY_W:GDN_IN_W].T
        g_rows, cols = _gdn_gates(ab_t, a_log[i], a_dt_bias[i], bsz, seq)
        o = _gdn_core(proj, a_conv_w[i].astype(f32), g_rows, cols, a_out_norm_g[i].astype(f32),
                      bsz, seq, f"gdn{i}_core")
        xf = _out_proj(o, a_w_out[i].astype(bf16), xf, final_norm_g, False, f"gdn{i}_out_proj")

    k, v_t, k_mean = _kv_proj(xf, kv_norm_g, w_kv.astype(bf16), bsz, seq)
    k_mean = k_mean.reshape(bsz, seq // MOBA_BLOCK, KEY_W)
    bias = _bias_tiles(rel_bias)

    n_b = b_w_in.shape[0]
    for j in range(n_b):
        qz = _norm_matmul(xf, b_norm_g[j], b_w_in[j].astype(bf16), MOBA_IN_TN, f"moba{j}_in_proj")
        o = _moba_attention(qz, k, v_t, k_mean, bias, bsz, seq, f"moba{j}_attn")
        xf = _out_proj(o, b_w_out[j].astype(bf16), xf, final_norm_g, j == n_b - 1, f"moba{j}_out_proj")
    return xf.reshape(bsz, seq, d).astype(x.dtype)
```

```python
import functools
import math

import jax
import jax.numpy as jnp
from jax import lax
from jax.experimental import pallas as pl
from jax.experimental.pallas import tpu as pltpu

f32 = jnp.float32
bf16 = jnp.bfloat16

D_MODEL = 1024
HEADS = 8
HEAD_DIM = 128
KEY_W = HEADS * HEAD_DIM
CONV_CH = 3 * KEY_W
CONV_W = 4
GDN_IN_W = CONV_CH + KEY_W + 2 * HEADS
CHUNK = 64
MOBA_BLOCK = 256
MOBA_TOPK = 3
N_BUCKETS = 32
MAX_DIST = 2048
EPS = 1e-6
NEG_INF = float("-inf")
LOG2E = math.log2(math.e)

LANES = 128
SUBLANES = 8
VMEM_LIMIT_BYTES = 56 * 1024 * 1024

MM_TM = 512
GDN_TN = 2048
MOBA_IN_TN = 1024
GDN_TB = 256
GDN_GROUP = 4
MOBA_UNROLL = 4
MOBA_HEADS_PER_STEP = 2
N_BIAS_TILES = MAX_DIST // MOBA_BLOCK + 2


def _cparams(sem):
    return pltpu.CompilerParams(dimension_semantics=sem, vmem_limit_bytes=VMEM_LIMIT_BYTES)


def _dot(a, b):
    return jnp.dot(a, b, preferred_element_type=f32)


def _dot_nt(a, b):
    return lax.dot_general(a, b, (((1,), (1,)), ((), ())), preferred_element_type=f32)


def _dot_tn(a, b):
    return lax.dot_general(a, b, (((0,), (0,)), ((), ())), preferred_element_type=f32)


def _split(x):
    hi = x.astype(bf16)
    lo = (x - hi.astype(f32)).astype(bf16)
    return hi, lo


def _rmsnorm_bf16(x_ref, g_ref):
    x = x_ref[...]
    y = x * lax.rsqrt(jnp.mean(x * x, axis=-1, keepdims=True) + EPS)
    return (y * g_ref[...]).astype(bf16)


def _cast_weights_once(step, w_ref, wb_ref):
    @pl.when(step == 0)
    def _():
        wb_ref[...] = w_ref[...].astype(bf16)


def _norm_mm_kernel(x_ref, g_ref, w_ref, o_ref, wb_ref):
    _cast_weights_once(pl.program_id(1), w_ref, wb_ref)
    o_ref[...] = _dot(_rmsnorm_bf16(x_ref, g_ref), wb_ref[...]).astype(o_ref.dtype)


def _norm_mm_side_kernel(x_ref, g_ref, w_ref, ws_ref, o_ref, os_ref, wb_ref):
    _cast_weights_once(pl.program_id(1), w_ref, wb_ref)
    xn = _rmsnorm_bf16(x_ref, g_ref)
    o_ref[...] = _dot(xn, wb_ref[...]).astype(o_ref.dtype)
    os_ref[...] = _dot(xn, ws_ref[...].astype(bf16))


def _norm_matmul(x, g, w_stack, layer, n, tn, out_dtype, name, w_side=None):
    m, k = x.shape
    in_specs = [
        pl.BlockSpec((MM_TM, k), lambda j, i: (i, 0)),
        pl.BlockSpec((1, k), lambda j, i: (0, 0)),
        pl.BlockSpec((None, k, tn), lambda j, i: (layer, 0, j)),
    ]
    out_specs = pl.BlockSpec((MM_TM, tn), lambda j, i: (i, j))
    out_shape = jax.ShapeDtypeStruct((m, n), out_dtype)
    args = [x, g.reshape(1, k).astype(f32), w_stack]
    body = _norm_mm_kernel
    if w_side is not None:
        ns = w_side.shape[1]
        in_specs.append(pl.BlockSpec((k, ns), lambda j, i: (0, 0)))
        out_specs = [out_specs, pl.BlockSpec((MM_TM, ns), lambda j, i: (i, 0))]
        out_shape = [out_shape, jax.ShapeDtypeStruct((m, ns), f32)]
        args.append(w_side)
        body = _norm_mm_side_kernel
    return pl.pallas_call(
        body,
        grid=(n // tn, m // MM_TM),
        in_specs=in_specs,
        out_specs=out_specs,
        out_shape=out_shape,
        scratch_shapes=[pltpu.VMEM((k, tn), bf16)],
        compiler_params=_cparams(("arbitrary", "arbitrary")),
        name=name,
    )(*args)


def _kv_kernel(x_ref, g_ref, w_ref, k_ref, vt_ref, km_ref, wb_ref):
    _cast_weights_once(pl.program_id(0), w_ref, wb_ref)
    acc = _dot(_rmsnorm_bf16(x_ref, g_ref), wb_ref[...])
    k = acc[:, :KEY_W]
    k_ref[...] = k.astype(bf16)
    km_ref[0] = jnp.mean(k, axis=0, keepdims=True)
    vt_ref[0] = acc[:, KEY_W:].T.astype(bf16)


def _kv_proj(x, g, w, bsz, seq):
    m, k = x.shape
    nblk = seq // MOBA_BLOCK
    return pl.pallas_call(
        _kv_kernel,
        grid=(m // MOBA_BLOCK,),
        in_specs=[
            pl.BlockSpec((MOBA_BLOCK, k), lambda i: (i, 0)),
            pl.BlockSpec((1, k), lambda i: (0, 0)),
            pl.BlockSpec((k, 2 * KEY_W), lambda i: (0, 0)),
        ],
        out_specs=[
            pl.BlockSpec((MOBA_BLOCK, KEY_W), lambda i: (i, 0)),
            pl.BlockSpec((1, KEY_W, MOBA_BLOCK), lambda i: (i // nblk, 0, i % nblk)),
            pl.BlockSpec((1, 1, KEY_W), lambda i: (i, 0, 0)),
        ],
        out_shape=[
            jax.ShapeDtypeStruct((m, KEY_W), bf16),
            jax.ShapeDtypeStruct((bsz, KEY_W, seq), bf16),
            jax.ShapeDtypeStruct((m // MOBA_BLOCK, 1, KEY_W), f32),
        ],
        scratch_shapes=[pltpu.VMEM((k, 2 * KEY_W), bf16)],
        compiler_params=_cparams(("arbitrary",)),
        name="kv_proj",
    )(x, g.reshape(1, k).astype(f32), w)


def _out_kernel(a_ref, w_ref, r_ref, g_ref, o_ref, wb_ref, *, final_norm):
    _cast_weights_once(pl.program_id(0), w_ref, wb_ref)
    y = r_ref[...] + _dot(a_ref[...], wb_ref[...])
    if final_norm:
        y = y * lax.rsqrt(jnp.mean(y * y, axis=-1, keepdims=True) + EPS) * g_ref[...]
    o_ref[...] = y


def _out_proj(a, w_stack, layer, res, g, final_norm, name):
    m, k = a.shape
    n = w_stack.shape[2]
    return pl.pallas_call(
        functools.partial(_out_kernel, final_norm=final_norm),
        grid=(m // MM_TM,),
        in_specs=[
            pl.BlockSpec((MM_TM, k), lambda i: (i, 0)),
            pl.BlockSpec((None, k, n), lambda i: (layer, 0, 0)),
            pl.BlockSpec((MM_TM, n), lambda i: (i, 0)),
            pl.BlockSpec((1, n), lambda i: (0, 0)),
        ],
        out_specs=pl.BlockSpec((MM_TM, n), lambda i: (i, 0)),
        out_shape=jax.ShapeDtypeStruct((m, n), f32),
        scratch_shapes=[pltpu.VMEM((k, n), bf16)],
        compiler_params=_cparams(("arbitrary",)),
        name=name,
    )(a, w_stack, res, g.reshape(1, n).astype(f32))


def _gates_kernel(a_ref, b_ref, alog_ref, dt_ref, g_ref, beta_ref, eg_ref, er_ref):
    x = a_ref[...] + dt_ref[...]
    softplus = jnp.maximum(x, 0.0) + jnp.log1p(jnp.exp(-jnp.abs(x)))
    g = -jnp.exp(alog_ref[...]) * softplus
    pos = lax.broadcasted_iota(jnp.int32, g.shape, 1) % CHUNK
    fwd = g
    bwd = g
    s = 1
    while s < CHUNK:
        fwd = fwd + jnp.where(pos >= s, pltpu.roll(fwd, s, 1), 0.0)
        bwd = bwd + jnp.where(pos < CHUNK - s, pltpu.roll(bwd, LANES - s, 1), 0.0)
        s *= 2
    g_ref[...] = fwd
    beta_ref[...] = jax.nn.sigmoid(b_ref[...])
    eg_ref[...] = jnp.exp(fwd)
    er_ref[...] = jnp.exp(bwd - g)


def _gdn_gates(ab_t, a_log, dt_bias, bsz, seq):
    m = bsz * seq
    rows = HEADS * m // LANES
    a2 = ab_t[:HEADS].reshape(rows, LANES)
    b2 = ab_t[HEADS:].reshape(rows, LANES)
    alog = jnp.repeat(a_log.astype(f32), m // LANES).reshape(rows, 1)
    dtb = jnp.repeat(dt_bias.astype(f32), m // LANES).reshape(rows, 1)
    full = pl.BlockSpec((rows, LANES), lambda: (0, 0))
    colspec = pl.BlockSpec((rows, 1), lambda: (0, 0))
    shp = jax.ShapeDtypeStruct((rows, LANES), f32)
    g, beta, eg, er = pl.pallas_call(
        _gates_kernel,
        in_specs=[full, full, colspec, colspec],
        out_specs=[full, full, full, full],
        out_shape=[shp, shp, shp, shp],
        compiler_params=pltpu.CompilerParams(vmem_limit_bytes=VMEM_LIMIT_BYTES),
        name="gdn_gates",
    )(a2, b2, alog, dtb)
    nck = seq // CHUNK
    g4 = g.reshape(HEADS, bsz, nck, CHUNK)
    g_rows = g4.transpose(1, 2, 0, 3)
    cols = jnp.stack([t.reshape(HEADS, bsz, nck, CHUNK) for t in (g, beta, eg, er)])
    cols = cols.transpose(2, 3, 4, 0, 1).reshape(bsz, nck, CHUNK, 4 * HEADS)
    return g_rows, cols


def _neumann_inverse(mats):
    n = mats[0].shape[0]
    row = lax.broadcasted_iota(jnp.int32, (n, n), 0)
    col = lax.broadcasted_iota(jnp.int32, (n, n), 1)
    eye = jnp.where(row == col, 1.0, 0.0).astype(f32)

    ts = [eye - a for a in mats]
    xbs = [a.astype(bf16) for a in mats]
    p = 1
    while True:
        xbs = [_dot(xb, xb).astype(bf16) for xb in xbs]
        p *= 2
        ts = [t + _dot(t.astype(bf16), xb) for t, xb in zip(ts, xbs)]
        if 2 * p >= n:
            return ts


def _gdn_kernel(x_ref, z_ref, cw_ref, grow_ref, cols_ref, og_ref, o_ref,
                stage, qkvn, state):
    tb = x_ref.shape[0]
    t = pl.program_id(1)

    @pl.when(t == 0)
    def _():
        stage[0:SUBLANES, :] = jnp.zeros((SUBLANES, CONV_CH), f32)
        state[...] = jnp.zeros_like(state)

    stage[SUBLANES:SUBLANES + tb, :] = x_ref[...].astype(f32)
    for cb in range(CONV_CH // LANES):
        cs = slice(cb * LANES, (cb + 1) * LANES)
        acc = stage[SUBLANES:SUBLANES + tb, cs] * cw_ref[CONV_W - 1:CONV_W, cs]
        for j in range(CONV_W - 1):
            lo = SUBLANES - (CONV_W - 1) + j
            acc = acc + stage[lo:lo + tb, cs] * cw_ref[j:j + 1, cs]
        y = acc * jax.nn.sigmoid(acc)
        if cb < 2 * HEADS:
            y = y * lax.rsqrt(jnp.sum(y * y, axis=-1, keepdims=True) + EPS)
            if cb < HEADS:
                y = y * (HEAD_DIM ** -0.5)
        qkvn[:, cs] = y
    stage[0:SUBLANES, :] = stage[tb:tb + SUBLANES, :]

    row = lax.broadcasted_iota(jnp.int32, (CHUNK, CHUNK), 0)
    col = lax.broadcasted_iota(jnp.int32, (CHUNK, CHUNK), 1)
    tril = row >= col
    strict = row > col
    og = og_ref[...]

    hs = range(HEADS)
    qcols = [slice(h * HEAD_DIM, (h + 1) * HEAD_DIM) for h in hs]

    def group_step(gi, carry):
        items = [(ci, h) for ci in range(GDN_GROUP) for h in hs]
        rows, g_rows, ctile = [], [], []
        for ci in range(GDN_GROUP):
            c = gi * GDN_GROUP + ci
            rows.append(pl.ds(pl.multiple_of(c * CHUNK, CHUNK), CHUNK))
            g_rows.append(grow_ref[0, c])
            ctile.append(cols_ref[0, c])

        def col(ci, h, which):
            return ctile[ci][:, which * HEADS + h:which * HEADS + h + 1]

        q = {it: qkvn[rows[it[0]], qcols[it[1]]] for it in items}
        k = {(ci, h): qkvn[rows[ci], KEY_W + h * HEAD_DIM:KEY_W + (h + 1) * HEAD_DIM] for ci, h in items}
        v = {(ci, h): qkvn[rows[ci], 2 * KEY_W + h * HEAD_DIM:2 * KEY_W + (h + 1) * HEAD_DIM]
             for ci, h in items}
        decay = {(ci, h): jnp.exp(jnp.where(tril, col(ci, h, 0) - g_rows[ci][h:h + 1, :], NEG_INF))
                 for ci, h in items}
        k_beta = {it: k[it] * col(*it, 1) for it in items}
        kq = {it: _dot_nt(jnp.concatenate([k_beta[it], q[it]], axis=0).astype(bf16), k[it].astype(bf16))
              for it in items}
        t_inv = dict(zip(items, _neumann_inverse(
            [jnp.where(strict, kq[it][:CHUNK] * decay[it], 0.0) for it in items])))
        uw = {it: _dot(t_inv[it].astype(bf16),
                       jnp.concatenate([v[it] * col(*it, 1), k_beta[it] * col(*it, 2)], axis=1).astype(bf16))
              for it in items}
        w_qd = {it: jnp.concatenate([uw[it][:, HEAD_DIM:], q[it] * col(*it, 2)], axis=0).astype(bf16)
                for it in items}
        qk = {it: (kq[it][CHUNK:] * decay[it]).astype(bf16) for it in items}
        k_dec = {it: (k[it] * col(*it, 3)).astype(bf16) for it in items}

        for ci in range(GDN_GROUP):
            s_prev = [state[h] for h in hs]
            ws_qs = [_dot(w_qd[ci, h], s_prev[h].astype(bf16)) for h in hs]
            v_new = [(uw[ci, h][:, :HEAD_DIM] - ws_qs[h][:CHUNK]).astype(bf16) for h in hs]
            o = [ws_qs[h][CHUNK:] + _dot(qk[ci, h], v_new[h]) for h in hs]
            for h in hs:
                g_last = col(ci, h, 2)[CHUNK - 1:CHUNK, :]
                state[h] = s_prev[h] * g_last + _dot_tn(k_dec[ci, h], v_new[h])
            for h in hs:
                on = o[h] * lax.rsqrt(jnp.mean(o[h] * o[h], axis=-1, keepdims=True) + EPS) * og
                z = z_ref[rows[ci], qcols[h]].astype(f32)
                o_ref[rows[ci], qcols[h]] = (on * (z * jax.nn.sigmoid(z))).astype(bf16)
        return carry

    lax.fori_loop(0, tb // (CHUNK * GDN_GROUP), group_step, 0)


def _gdn_core(proj, conv_w, g_rows, cols, out_norm_g, bsz, seq, name):
    m = bsz * seq
    nt = seq // GDN_TB
    cpt = GDN_TB // CHUNK
    return pl.pallas_call(
        _gdn_kernel,
        grid=(bsz, nt),
        in_specs=[
            pl.BlockSpec((GDN_TB, CONV_CH), lambda b, t: (b * nt + t, 0)),
            pl.BlockSpec((GDN_TB, KEY_W), lambda b, t: (b * nt + t, CONV_CH // KEY_W)),
            pl.BlockSpec((CONV_W, CONV_CH), lambda b, t: (0, 0)),
            pl.BlockSpec((1, cpt, HEADS, CHUNK), lambda b, t: (b, t, 0, 0)),
            pl.BlockSpec((1, cpt, CHUNK, 4 * HEADS), lambda b, t: (b, t, 0, 0)),
            pl.BlockSpec((1, HEAD_DIM), lambda b, t: (0, 0)),
        ],
        out_specs=pl.BlockSpec((GDN_TB, KEY_W), lambda b, t: (b * nt + t, 0)),
        out_shape=jax.ShapeDtypeStruct((m, KEY_W), bf16),
        scratch_shapes=[
            pltpu.VMEM((GDN_TB + SUBLANES, CONV_CH), f32),
            pltpu.VMEM((GDN_TB, CONV_CH), f32),
            pltpu.VMEM((HEADS, HEAD_DIM, HEAD_DIM), f32),
        ],
        compiler_params=_cparams(("parallel", "arbitrary")),
        name=name,
    )(proj, proj, conv_w, g_rows, cols, out_norm_g.reshape(1, HEAD_DIM))


def _bias_kernel(rb_ref, o_ref):
    h = pl.program_id(0)
    d = pl.program_id(1)
    key = lax.broadcasted_iota(jnp.int32, (MOBA_BLOCK, MOBA_BLOCK), 0)
    qry = lax.broadcasted_iota(jnp.int32, (MOBA_BLOCK, MOBA_BLOCK), 1)
    n = jnp.maximum(d * MOBA_BLOCK + qry - key, 0)
    max_exact = N_BUCKETS // 2
    nf = jnp.maximum(n, 1).astype(f32)
    large = max_exact + (jnp.log(nf / max_exact) / math.log(MAX_DIST / max_exact)
                         * (N_BUCKETS - max_exact)).astype(jnp.int32)
    large = jnp.minimum(large, N_BUCKETS - 1)
    bucket = jnp.where(n < max_exact, n, large)
    out = jnp.zeros((MOBA_BLOCK, MOBA_BLOCK), f32)
    for b in range(N_BUCKETS):
        out = jnp.where(bucket == b, rb_ref[h, b], out)
    o_ref[0, 0] = out * LOG2E


def _bias_tiles(rel_bias):
    return pl.pallas_call(
        _bias_kernel,
        grid=(HEADS, N_BIAS_TILES),
        in_specs=[pl.BlockSpec(memory_space=pltpu.SMEM)],
        out_specs=pl.BlockSpec((1, 1, MOBA_BLOCK, MOBA_BLOCK), lambda h, d: (h, d, 0, 0)),
        out_shape=jax.ShapeDtypeStruct((HEADS, N_BIAS_TILES, MOBA_BLOCK, MOBA_BLOCK), f32),
        compiler_params=_cparams(("parallel", "parallel")),
        name="t5_bias_tiles",
    )(rel_bias.T.astype(f32))


def _moba_kernel(q_ref, z_ref, k_ref, vt_ref, km_ref, bias_ref, o_ref, s_scr, smax_scr, p_scr):
    cur = pl.program_id(2)
    nblk = km_ref.shape[1]
    heads = range(MOBA_HEADS_PER_STEP)
    hcols = [slice(h * HEAD_DIM, (h + 1) * HEAD_DIM) for h in heads]
    q_t = [q_ref[:, cs].T for cs in hcols]

    blk = lax.broadcasted_iota(jnp.int32, (nblk, MOBA_BLOCK), 0).astype(f32)
    gates = []
    for h in heads:
        qh, ql = _split(q_t[h])
        kmh, kml = _split(km_ref[0, :, hcols[h]])
        gate = _dot(kmh, qh) + _dot(kml, qh) + _dot(kmh, ql)
        gates.append(jnp.where(blk < cur.astype(f32), gate, NEG_INF))
    sels = [[] for _ in heads]
    for _ in range(MOBA_TOPK):
        for h in heads:
            best = jnp.max(gates[h], axis=0, keepdims=True)
            idx = jnp.min(jnp.where(gates[h] == best, blk, float(nblk)), axis=0, keepdims=True)
            idx = jnp.where(best > NEG_INF, idx, -1.0)
            sels[h].append(idx)
            gates[h] = jnp.where(blk == idx, NEG_INF, gates[h])

    qb = [(q_t[h] * (HEAD_DIM ** -0.5 * LOG2E)).astype(bf16) for h in heads]

    def block_scores(h, j):
        jc = jnp.minimum(j, nblk - 1)
        dist = jnp.clip(cur - jc, 0, N_BIAS_TILES - 1)
        kj = k_ref[pl.ds(pl.multiple_of(jc * MOBA_BLOCK, MOBA_BLOCK), MOBA_BLOCK), hcols[h]]
        return _dot(kj, qb[h]) + bias_ref[h, dist]

    def block_values(h, j):
        jc = jnp.minimum(j, nblk - 1)
        return vt_ref[0, hcols[h], pl.ds(pl.multiple_of(jc * MOBA_BLOCK, MOBA_BLOCK), MOBA_BLOCK)]

    key = lax.broadcasted_iota(jnp.int32, (MOBA_BLOCK, MOBA_BLOCK), 0)
    qry = lax.broadcasted_iota(jnp.int32, (MOBA_BLOCK, MOBA_BLOCK), 1)
    init = []
    for h in heads:
        s = jnp.where(qry >= key, block_scores(h, cur), NEG_INF)
        m0 = jnp.max(s, axis=0, keepdims=True)
        p = jnp.exp2(s - m0)
        l0 = jnp.sum(p, axis=0, keepdims=True)
        acc0 = _dot(block_values(h, cur), p.astype(bf16))
        init.append((m0, l0, acc0))

    units = [(h, u) for h in heads for u in range(MOBA_UNROLL)]

    def stash_scores(h, u, j):
        s = block_scores(h, j)
        s_scr[h, u] = s
        smax_scr[h, u] = jnp.max(s, axis=0, keepdims=True)

    def pending_pv(h, first_block):
        pv = None
        for u in range(MOBA_UNROLL):
            t = _dot(block_values(h, jnp.maximum(first_block + u, 0)), p_scr[h, u])
            pv = t if pv is None else pv + t
        return pv

    for h, u in units:
        stash_scores(h, u, jnp.int32(u))
        p_scr[h, u] = jnp.zeros((MOBA_BLOCK, MOBA_BLOCK), bf16)

    def step(it, state):
        base = it * MOBA_UNROLL
        pv = [pending_pv(h, base - MOBA_UNROLL) for h in heads]
        out = []
        for h in heads:
            m_prev, l_prev, acc = state[h]
            chosen = []
            for u in range(MOBA_UNROLL):
                j = base + u
                jf = j.astype(f32)
                chosen.append(((sels[h][0] == jf) | (sels[h][1] == jf) | (sels[h][2] == jf))
                              & (j < cur))
            m_new = m_prev
            for u in range(MOBA_UNROLL):
                m_new = jnp.maximum(m_new, jnp.where(chosen[u], smax_scr[h, u], NEG_INF))
            alpha = jnp.exp2(m_prev - m_new)
            l_new = alpha * l_prev
            for u in range(MOBA_UNROLL):
                p = jnp.exp2(s_scr[h, u] - jnp.where(chosen[u], m_new, float("inf")))
                l_new = l_new + jnp.sum(p, axis=0, keepdims=True)
                p_scr[h, u] = p.astype(bf16)
            out.append((m_new, l_new, alpha * (acc + pv[h])))
        for h, u in units:
            stash_scores(h, u, base + MOBA_UNROLL + u)
        return tuple(out)

    n_steps = (cur + MOBA_UNROLL - 1) // MOBA_UNROLL
    state = lax.fori_loop(0, n_steps, step, tuple(init))
    for h in heads:
        _, l_fin, acc = state[h]
        acc = acc + pending_pv(h, (n_steps - 1) * MOBA_UNROLL)
        z = z_ref[:, hcols[h]]
        o_ref[:, hcols[h]] = ((acc / l_fin).T * (z * jax.nn.sigmoid(z))).astype(bf16)


def _moba_attention(qz, k, v_t, k_mean, bias, bsz, seq, name):
    m = bsz * seq
    nq = seq // MOBA_BLOCK
    hp = MOBA_HEADS_PER_STEP
    w = hp * HEAD_DIM
    return pl.pallas_call(
        _moba_kernel,
        grid=(bsz, HEADS // hp, nq),
        in_specs=[
            pl.BlockSpec((MOBA_BLOCK, w), lambda b, g, i: (b * nq + i, g)),
            pl.BlockSpec((MOBA_BLOCK, w), lambda b, g, i: (b * nq + i, HEADS // hp + g)),
            pl.BlockSpec((seq, w), lambda b, g, i: (b, g)),
            pl.BlockSpec((1, w, seq), lambda b, g, i: (b, g, 0)),
            pl.BlockSpec((1, nq, w), lambda b, g, i: (b, 0, g)),
            pl.BlockSpec((hp, N_BIAS_TILES, MOBA_BLOCK, MOBA_BLOCK), lambda b, g, i: (g, 0, 0, 0)),
        ],
        out_specs=pl.BlockSpec((MOBA_BLOCK, w), lambda b, g, i: (b * nq + i, g)),
        out_shape=jax.ShapeDtypeStruct((m, KEY_W), bf16),
        scratch_shapes=[pltpu.VMEM((hp, MOBA_UNROLL, MOBA_BLOCK, MOBA_BLOCK), f32),
                        pltpu.VMEM((hp, MOBA_UNROLL, 1, MOBA_BLOCK), f32),
                        pltpu.VMEM((hp, MOBA_UNROLL, MOBA_BLOCK, MOBA_BLOCK), bf16)],
        compiler_params=_cparams(("parallel", "parallel", "arbitrary")),
        name=name,
    )(qz, qz, k, v_t, k_mean, bias)


def kernel(x, a_norm_g, a_w_in, a_conv_w, a_log, a_dt_bias, a_out_norm_g, a_w_out, kv_norm_g, w_kv, b_norm_g, b_w_in, b_w_out, rel_bias, final_norm_g):
    bsz, seq, d = x.shape
    m = bsz * seq
    assert d == D_MODEL and seq % MOBA_BLOCK == 0 and seq % GDN_TB == 0 and m % MM_TM == 0
    xf = x.reshape(m, d).astype(f32)

    qkvz_w = CONV_CH + KEY_W
    for i in range(a_w_in.shape[0]):
        w_ab = jnp.pad(a_w_in[i, :, qkvz_w:], ((0, 0), (0, LANES - 2 * HEADS)))
        proj, ab = _norm_matmul(xf, a_norm_g[i], a_w_in, i, qkvz_w, GDN_TN, bf16,
                                f"gdn{i}_in_proj", w_side=w_ab)
        g_rows, cols = _gdn_gates(ab[:, :2 * HEADS].T, a_log[i], a_dt_bias[i], bsz, seq)
        o = _gdn_core(proj, a_conv_w[i].astype(f32), g_rows, cols, a_out_norm_g[i].astype(f32),
                      bsz, seq, f"gdn{i}_core")
        xf = _out_proj(o, a_w_out, i, xf, final_norm_g, False, f"gdn{i}_out_proj")

    k, v_t, k_mean = _kv_proj(xf, kv_norm_g, w_kv, bsz, seq)
    k_mean = k_mean.reshape(bsz, seq // MOBA_BLOCK, KEY_W)
    bias = _bias_tiles(rel_bias)

    n_b = b_w_in.shape[0]
    for j in range(n_b):
        qz = _norm_matmul(xf, b_norm_g[j], b_w_in, j, 2 * KEY_W, MOBA_IN_TN, f32, f"moba{j}_in_proj")
        o = _moba_attention(qz, k, v_t, k_mean, bias, bsz, seq, f"moba{j}_attn")
        xf = _out_proj(o, b_w_out, j, xf, final_norm_g, j == n_b - 1, f"moba{j}_out_proj")
    return xf.reshape(bsz, seq, d).astype(x.dtype)
```

```python
import functools
import math

import jax
import jax.numpy as jnp
from jax import lax
from jax.experimental import pallas as pl
from jax.experimental.pallas import tpu as pltpu

f32 = jnp.float32
bf16 = jnp.bfloat16

D_MODEL = 1024
HEADS = 8
HEAD_DIM = 128
KEY_W = HEADS * HEAD_DIM
CONV_CH = 3 * KEY_W
CONV_W = 4
GDN_IN_W = CONV_CH + KEY_W + 2 * HEADS
CHUNK = 64
MOBA_BLOCK = 256
MOBA_TOPK = 3
N_BUCKETS = 32
MAX_DIST = 2048
EPS = 1e-6
NEG_INF = float("-inf")
LOG2E = math.log2(math.e)

LANES = 128
SUBLANES = 8
VMEM_LIMIT_BYTES = 56 * 1024 * 1024

MM_TM = 1024
GDN_TN = 2048
MOBA_IN_TN = 1024
GATES_TM = 2048
GDN_TB = 256
GDN_GROUP = 4
MOBA_UNROLL = 4
MOBA_HEADS_PER_STEP = 2
N_BIAS_TILES = MAX_DIST // MOBA_BLOCK + 2


def _cparams(sem):
    return pltpu.CompilerParams(dimension_semantics=sem, vmem_limit_bytes=VMEM_LIMIT_BYTES)


def _dot(a, b):
    return jnp.dot(a, b, preferred_element_type=f32)


def _dot_nt(a, b):
    return lax.dot_general(a, b, (((1,), (1,)), ((), ())), preferred_element_type=f32)


def _dot_tn(a, b):
    return lax.dot_general(a, b, (((0,), (0,)), ((), ())), preferred_element_type=f32)


def _split(x):
    hi = x.astype(bf16)
    lo = (x - hi.astype(f32)).astype(bf16)
    return hi, lo


def _rmsnorm_bf16(x_ref, g_ref):
    x = x_ref[...]
    y = x * lax.rsqrt(jnp.mean(x * x, axis=-1, keepdims=True) + EPS)
    return (y * g_ref[...]).astype(bf16)


def _cast_weights_once(step, w_ref, wb_ref):
    @pl.when(step == 0)
    def _():
        wb_ref[...] = w_ref[...].astype(bf16)


def _norm_mm_kernel(x_ref, g_ref, w_ref, o_ref, wb_ref):
    _cast_weights_once(pl.program_id(1), w_ref, wb_ref)
    o_ref[...] = _dot(_rmsnorm_bf16(x_ref, g_ref), wb_ref[...]).astype(o_ref.dtype)


def _norm_mm_side_kernel(x_ref, g_ref, w_ref, ws_ref, o_ref, os_ref, wb_ref):
    _cast_weights_once(pl.program_id(1), w_ref, wb_ref)
    xn = _rmsnorm_bf16(x_ref, g_ref)
    o_ref[...] = _dot(xn, wb_ref[...]).astype(o_ref.dtype)
    os_ref[...] = _dot(xn, ws_ref[...].astype(bf16))


def _norm_matmul(x, g, w_stack, layer, n, tn, out_dtype, name, col0=0, w_side=None):
    m, k = x.shape
    in_specs = [
        pl.BlockSpec((MM_TM, k), lambda j, i: (i, 0)),
        pl.BlockSpec((1, k), lambda j, i: (0, 0)),
        pl.BlockSpec((None, k, tn), lambda j, i: (layer, 0, col0 + j)),
    ]
    out_specs = pl.BlockSpec((MM_TM, tn), lambda j, i: (i, j))
    out_shape = jax.ShapeDtypeStruct((m, n), out_dtype)
    args = [x, g.reshape(1, k).astype(f32), w_stack]
    body = _norm_mm_kernel
    if w_side is not None:
        ns = w_side.shape[1]
        in_specs.append(pl.BlockSpec((k, ns), lambda j, i: (0, 0)))
        out_specs = [out_specs, pl.BlockSpec((None, MM_TM, ns), lambda j, i: (j, i, 0))]
        out_shape = [out_shape, jax.ShapeDtypeStruct((n // tn, m, ns), f32)]
        args.append(w_side)
        body = _norm_mm_side_kernel
    return pl.pallas_call(
        body,
        grid=(n // tn, m // MM_TM),
        in_specs=in_specs,
        out_specs=out_specs,
        out_shape=out_shape,
        scratch_shapes=[pltpu.VMEM((k, tn), bf16)],
        compiler_params=_cparams(("arbitrary", "arbitrary")),
        name=name,
    )(*args)


def _kv_kernel(x_ref, g_ref, w_ref, k_ref, vt_ref, km_ref, wb_ref):
    _cast_weights_once(pl.program_id(0), w_ref, wb_ref)
    acc = _dot(_rmsnorm_bf16(x_ref, g_ref), wb_ref[...])
    k = acc[:, :KEY_W]
    k_ref[...] = k.astype(bf16)
    km_ref[0] = jnp.mean(k, axis=0, keepdims=True)
    vt_ref[0] = acc[:, KEY_W:].T.astype(bf16)


def _kv_proj(x, g, w, bsz, seq):
    m, k = x.shape
    nblk = seq // MOBA_BLOCK
    return pl.pallas_call(
        _kv_kernel,
        grid=(m // MOBA_BLOCK,),
        in_specs=[
            pl.BlockSpec((MOBA_BLOCK, k), lambda i: (i, 0)),
            pl.BlockSpec((1, k), lambda i: (0, 0)),
            pl.BlockSpec((k, 2 * KEY_W), lambda i: (0, 0)),
        ],
        out_specs=[
            pl.BlockSpec((MOBA_BLOCK, KEY_W), lambda i: (i, 0)),
            pl.BlockSpec((1, KEY_W, MOBA_BLOCK), lambda i: (i // nblk, 0, i % nblk)),
            pl.BlockSpec((1, 1, KEY_W), lambda i: (i, 0, 0)),
        ],
        out_shape=[
            jax.ShapeDtypeStruct((m, KEY_W), bf16),
            jax.ShapeDtypeStruct((bsz, KEY_W, seq), bf16),
            jax.ShapeDtypeStruct((m // MOBA_BLOCK, 1, KEY_W), f32),
        ],
        scratch_shapes=[pltpu.VMEM((k, 2 * KEY_W), bf16)],
        compiler_params=_cparams(("arbitrary",)),
        name="kv_proj",
    )(x, g.reshape(1, k).astype(f32), w)


def _out_kernel(a_ref, w_ref, r_ref, g_ref, o_ref, wb_ref, *, final_norm):
    _cast_weights_once(pl.program_id(0), w_ref, wb_ref)
    y = r_ref[...] + _dot(a_ref[...], wb_ref[...])
    if final_norm:
        y = y * lax.rsqrt(jnp.mean(y * y, axis=-1, keepdims=True) + EPS) * g_ref[...]
    o_ref[...] = y


def _out_proj(a, w_stack, layer, res, g, final_norm, name):
    m, k = a.shape
    n = w_stack.shape[2]
    return pl.pallas_call(
        functools.partial(_out_kernel, final_norm=final_norm),
        grid=(m // MM_TM,),
        in_specs=[
            pl.BlockSpec((MM_TM, k), lambda i: (i, 0)),
            pl.BlockSpec((None, k, n), lambda i: (layer, 0, 0)),
            pl.BlockSpec((MM_TM, n), lambda i: (i, 0)),
            pl.BlockSpec((1, n), lambda i: (0, 0)),
        ],
        out_specs=pl.BlockSpec((MM_TM, n), lambda i: (i, 0)),
        out_shape=jax.ShapeDtypeStruct((m, n), f32),
        scratch_shapes=[pltpu.VMEM((k, n), bf16)],
        compiler_params=_cparams(("arbitrary",)),
        name=name,
    )(a, w_stack, res, g.reshape(1, n).astype(f32))


def _gates_kernel(ab_ref, alog_ref, dt_ref, o_ref):
    ab = ab_ref[...]
    rows = ab.shape[0]
    x = ab + dt_ref[...]
    softplus = jnp.maximum(x, 0.0) + jnp.log1p(jnp.exp(-jnp.abs(x)))
    g = -jnp.exp(alog_ref[...]) * softplus
    pos = lax.broadcasted_iota(jnp.int32, ab.shape, 0) % CHUNK
    fwd = g
    bwd = g
    s = 1
    while s < CHUNK:
        fwd = fwd + jnp.where(pos >= s, pltpu.roll(fwd, s, 0), 0.0)
        bwd = bwd + jnp.where(pos < CHUNK - s, pltpu.roll(bwd, rows - s, 0), 0.0)
        s *= 2
    lane = lax.broadcasted_iota(jnp.int32, ab.shape, 1)
    out = jnp.where(lane < HEADS, fwd, jax.nn.sigmoid(ab))
    out = jnp.where(lane < 2 * HEADS, out, pltpu.roll(jnp.exp(fwd), 2 * HEADS, 1))
    out = jnp.where(lane < 3 * HEADS, out, pltpu.roll(jnp.exp(bwd - g), 3 * HEADS, 1))
    o_ref[...] = jnp.where(lane < 4 * HEADS, out, 0.0)


def _gdn_gates(ab, a_log, dt_bias):
    m = ab.shape[0]
    lane_pad = (0, LANES - HEADS)
    alog = jnp.pad(a_log.astype(f32), lane_pad).reshape(1, LANES)
    dtb = jnp.pad(dt_bias.astype(f32), lane_pad).reshape(1, LANES)
    tile = pl.BlockSpec((GATES_TM, LANES), lambda i: (i, 0))
    vec = pl.BlockSpec((1, LANES), lambda i: (0, 0))
    return pl.pallas_call(
        _gates_kernel,
        grid=(m // GATES_TM,),
        in_specs=[tile, vec, vec],
        out_specs=tile,
        out_shape=jax.ShapeDtypeStruct((m, LANES), f32),
        compiler_params=_cparams(("parallel",)),
        name="gdn_gates",
    )(ab, alog, dtb)


def _neumann_inverse(mats):
    n = mats[0].shape[0]
    row = lax.broadcasted_iota(jnp.int32, (n, n), 0)
    col = lax.broadcasted_iota(jnp.int32, (n, n), 1)
    eye = jnp.where(row == col, 1.0, 0.0).astype(f32)

    ts = [eye - a for a in mats]
    xbs = [a.astype(bf16) for a in mats]
    p = 1
    while True:
        xbs = [_dot(xb, xb).astype(bf16) for xb in xbs]
        p *= 2
        ts = [t + _dot(t.astype(bf16), xb) for t, xb in zip(ts, xbs)]
        if 2 * p >= n:
            return ts


def _gdn_kernel(x_ref, z_ref, cw_ref, cols_ref, og_ref, o_ref,
                stage, qkvn, state):
    tb = x_ref.shape[0]
    t = pl.program_id(1)

    @pl.when(t == 0)
    def _():
        stage[0:SUBLANES, :] = jnp.zeros((SUBLANES, CONV_CH), f32)
        state[...] = jnp.zeros_like(state)

    stage[SUBLANES:SUBLANES + tb, :] = x_ref[...].astype(f32)
    for cb in range(CONV_CH // LANES):
        cs = slice(cb * LANES, (cb + 1) * LANES)
        acc = stage[SUBLANES:SUBLANES + tb, cs] * cw_ref[CONV_W - 1:CONV_W, cs]
        for j in range(CONV_W - 1):
            lo = SUBLANES - (CONV_W - 1) + j
            acc = acc + stage[lo:lo + tb, cs] * cw_ref[j:j + 1, cs]
        y = acc * jax.nn.sigmoid(acc)
        if cb < 2 * HEADS:
            y = y * lax.rsqrt(jnp.sum(y * y, axis=-1, keepdims=True) + EPS)
            if cb < HEADS:
                y = y * (HEAD_DIM ** -0.5)
        qkvn[:, cs] = y
    stage[0:SUBLANES, :] = stage[tb:tb + SUBLANES, :]

    row = lax.broadcasted_iota(jnp.int32, (CHUNK, CHUNK), 0)
    col = lax.broadcasted_iota(jnp.int32, (CHUNK, CHUNK), 1)
    tril = row >= col
    strict = row > col
    og = og_ref[...]

    hs = range(HEADS)
    qcols = [slice(h * HEAD_DIM, (h + 1) * HEAD_DIM) for h in hs]

    def group_step(gi, carry):
        items = [(ci, h) for ci in range(GDN_GROUP) for h in hs]
        rows, g_rows, ctile = [], [], []
        for ci in range(GDN_GROUP):
            c = gi * GDN_GROUP + ci
            rows.append(pl.ds(pl.multiple_of(c * CHUNK, CHUNK), CHUNK))
            ctile.append(cols_ref[0, c])
            g_rows.append(ctile[ci].T)

        def col(ci, h, which):
            return ctile[ci][:, which * HEADS + h:which * HEADS + h + 1]

        q = {it: qkvn[rows[it[0]], qcols[it[1]]] for it in items}
        k = {(ci, h): qkvn[rows[ci], KEY_W + h * HEAD_DIM:KEY_W + (h + 1) * HEAD_DIM] for ci, h in items}
        v = {(ci, h): qkvn[rows[ci], 2 * KEY_W + h * HEAD_DIM:2 * KEY_W + (h + 1) * HEAD_DIM]
             for ci, h in items}
        decay = {(ci, h): jnp.exp(jnp.where(tril, col(ci, h, 0) - g_rows[ci][h:h + 1, :], NEG_INF))
                 for ci, h in items}
        k_beta = {it: k[it] * col(*it, 1) for it in items}
        kq = {it: _dot_nt(jnp.concatenate([k_beta[it], q[it]], axis=0).astype(bf16), k[it].astype(bf16))
              for it in items}
        t_inv = dict(zip(items, _neumann_inverse(
            [jnp.where(strict, kq[it][:CHUNK] * decay[it], 0.0) for it in items])))
        uw = {it: _dot(t_inv[it].astype(bf16),
                       jnp.concatenate([v[it] * col(*it, 1), k_beta[it] * col(*it, 2)], axis=1).astype(bf16))
              for it in items}
        w_qd = {it: jnp.concatenate([uw[it][:, HEAD_DIM:], q[it] * col(*it, 2)], axis=0).astype(bf16)
                for it in items}
        qk = {it: (kq[it][CHUNK:] * decay[it]).astype(bf16) for it in items}
        k_dec = {it: (k[it] * col(*it, 3)).astype(bf16) for it in items}

        for ci in range(GDN_GROUP):
            s_prev = [state[h] for h in hs]
            ws_qs = [_dot(w_qd[ci, h], s_prev[h].astype(bf16)) for h in hs]
            v_new = [(uw[ci, h][:, :HEAD_DIM] - ws_qs[h][:CHUNK]).astype(bf16) for h in hs]
            o = [ws_qs[h][CHUNK:] + _dot(qk[ci, h], v_new[h]) for h in hs]
            for h in hs:
                g_last = col(ci, h, 2)[CHUNK - 1:CHUNK, :]
                state[h] = s_prev[h] * g_last + _dot_tn(k_dec[ci, h], v_new[h])
            for h in hs:
                on = o[h] * lax.rsqrt(jnp.mean(o[h] * o[h], axis=-1, keepdims=True) + EPS) * og
                z = z_ref[rows[ci], qcols[h]].astype(f32)
                o_ref[rows[ci], qcols[h]] = (on * (z * jax.nn.sigmoid(z))).astype(bf16)
        return carry

    lax.fori_loop(0, tb // (CHUNK * GDN_GROUP), group_step, 0)


def _gdn_core(proj, conv_w, cols, out_norm_g, bsz, seq, name):
    m = bsz * seq
    nt = seq // GDN_TB
    cpt = GDN_TB // CHUNK
    return pl.pallas_call(
        _gdn_kernel,
        grid=(bsz, nt),
        in_specs=[
            pl.BlockSpec((GDN_TB, CONV_CH), lambda b, t: (b * nt + t, 0)),
            pl.BlockSpec((GDN_TB, KEY_W), lambda b, t: (b * nt + t, CONV_CH // KEY_W)),
            pl.BlockSpec((CONV_W, CONV_CH), lambda b, t: (0, 0)),
            pl.BlockSpec((1, cpt, CHUNK, LANES), lambda b, t: (b, t, 0, 0)),
            pl.BlockSpec((1, HEAD_DIM), lambda b, t: (0, 0)),
        ],
        out_specs=pl.BlockSpec((GDN_TB, KEY_W), lambda b, t: (b * nt + t, 0)),
        out_shape=jax.ShapeDtypeStruct((m, KEY_W), bf16),
        scratch_shapes=[
            pltpu.VMEM((GDN_TB + SUBLANES, CONV_CH), f32),
            pltpu.VMEM((GDN_TB, CONV_CH), f32),
            pltpu.VMEM((HEADS, HEAD_DIM, HEAD_DIM), f32),
        ],
        compiler_params=_cparams(("parallel", "arbitrary")),
        name=name,
    )(proj, proj, conv_w, cols.reshape(bsz, seq // CHUNK, CHUNK, LANES), out_norm_g.reshape(1, HEAD_DIM))


def _bias_kernel(rb_ref, o_ref):
    h = pl.program_id(0)
    d = pl.program_id(1)
    key = lax.broadcasted_iota(jnp.int32, (MOBA_BLOCK, MOBA_BLOCK), 0)
    qry = lax.broadcasted_iota(jnp.int32, (MOBA_BLOCK, MOBA_BLOCK), 1)
    n = jnp.maximum(d * MOBA_BLOCK + qry - key, 0)
    max_exact = N_BUCKETS // 2
    nf = jnp.maximum(n, 1).astype(f32)
    large = max_exact + (jnp.log(nf / max_exact) / math.log(MAX_DIST / max_exact)
                         * (N_BUCKETS - max_exact)).astype(jnp.int32)
    large = jnp.minimum(large, N_BUCKETS - 1)
    bucket = jnp.where(n < max_exact, n, large)
    out = jnp.zeros((MOBA_BLOCK, MOBA_BLOCK), f32)
    for b in range(N_BUCKETS):
        out = jnp.where(bucket == b, rb_ref[h, b], out)
    o_ref[0, 0] = out * LOG2E


def _bias_tiles(rel_bias):
    return pl.pallas_call(
        _bias_kernel,
        grid=(HEADS, N_BIAS_TILES),
        in_specs=[pl.BlockSpec(memory_space=pltpu.SMEM)],
        out_specs=pl.BlockSpec((1, 1, MOBA_BLOCK, MOBA_BLOCK), lambda h, d: (h, d, 0, 0)),
        out_shape=jax.ShapeDtypeStruct((HEADS, N_BIAS_TILES, MOBA_BLOCK, MOBA_BLOCK), f32),
        compiler_params=_cparams(("parallel", "parallel")),
        name="t5_bias_tiles",
    )(rel_bias.T.astype(f32))


def _moba_kernel(q_ref, z_ref, k_ref, vt_ref, km_ref, bias_ref, o_ref, s_scr, smax_scr, p_scr):
    cur = pl.program_id(2)
    nblk = km_ref.shape[1]
    heads = range(MOBA_HEADS_PER_STEP)
    hcols = [slice(h * HEAD_DIM, (h + 1) * HEAD_DIM) for h in heads]
    q_t = [q_ref[:, cs].T for cs in hcols]

    blk = lax.broadcasted_iota(jnp.int32, (nblk, MOBA_BLOCK), 0).astype(f32)
    gates = []
    for h in heads:
        qh, ql = _split(q_t[h])
        kmh, kml = _split(km_ref[0, :, hcols[h]])
        gate = _dot(kmh, qh) + _dot(kml, qh) + _dot(kmh, ql)
        gates.append(jnp.where(blk < cur.astype(f32), gate, NEG_INF))
    sels = [[] for _ in heads]
    for _ in range(MOBA_TOPK):
        for h in heads:
            best = jnp.max(gates[h], axis=0, keepdims=True)
            idx = jnp.min(jnp.where(gates[h] == best, blk, float(nblk)), axis=0, keepdims=True)
            idx = jnp.where(best > NEG_INF, idx, -1.0)
            sels[h].append(idx)
            gates[h] = jnp.where(blk == idx, NEG_INF, gates[h])

    qb = [(q_t[h] * (HEAD_DIM ** -0.5 * LOG2E)).astype(bf16) for h in heads]

    def block_scores(h, j):
        jc = jnp.minimum(j, nblk - 1)
        dist = jnp.clip(cur - jc, 0, N_BIAS_TILES - 1)
        kj = k_ref[pl.ds(pl.multiple_of(jc * MOBA_BLOCK, MOBA_BLOCK), MOBA_BLOCK), hcols[h]]
        return _dot(kj, qb[h]) + bias_ref[h, dist]

    def block_values(h, j):
        jc = jnp.minimum(j, nblk - 1)
        return vt_ref[0, hcols[h], pl.ds(pl.multiple_of(jc * MOBA_BLOCK, MOBA_BLOCK), MOBA_BLOCK)]

    key = lax.broadcasted_iota(jnp.int32, (MOBA_BLOCK, MOBA_BLOCK), 0)
    qry = lax.broadcasted_iota(jnp.int32, (MOBA_BLOCK, MOBA_BLOCK), 1)
    init = []
    for h in heads:
        s = jnp.where(qry >= key, block_scores(h, cur), NEG_INF)
        m0 = jnp.max(s, axis=0, keepdims=True)
        p = jnp.exp2(s - m0)
        l0 = jnp.sum(p, axis=0, keepdims=True)
        acc0 = _dot(block_values(h, cur), p.astype(bf16))
        init.append((m0, l0, acc0))

    units = [(h, u) for h in heads for u in range(MOBA_UNROLL)]

    def stash_scores(h, u, j):
        s = block_scores(h, j)
        s_scr[h, u] = s
        smax_scr[h, u] = jnp.max(s, axis=0, keepdims=True)

    def pending_pv(h, first_block):
        pv = None
        for u in range(MOBA_UNROLL):
            t = _dot(block_values(h, jnp.maximum(first_block + u, 0)), p_scr[h, u])
            pv = t if pv is None else pv + t
        return pv

    for h, u in units:
        stash_scores(h, u, jnp.int32(u))
        p_scr[h, u] = jnp.zeros((MOBA_BLOCK, MOBA_BLOCK), bf16)

    def step(it, state):
        base = it * MOBA_UNROLL
        pv = [pending_pv(h, base - MOBA_UNROLL) for h in heads]
        out = []
        for h in heads:
            m_prev, l_prev, acc = state[h]
            chosen = []
            for u in range(MOBA_UNROLL):
                j = base + u
                jf = j.astype(f32)
                chosen.append(((sels[h][0] == jf) | (sels[h][1] == jf) | (sels[h][2] == jf))
                              & (j < cur))
            m_new = m_prev
            for u in range(MOBA_UNROLL):
                m_new = jnp.maximum(m_new, jnp.where(chosen[u], smax_scr[h, u], NEG_INF))
            alpha = jnp.exp2(m_prev - m_new)
            l_new = alpha * l_prev
            for u in range(MOBA_UNROLL):
                p = jnp.exp2(s_scr[h, u] - jnp.where(chosen[u], m_new, float("inf")))
                l_new = l_new + jnp.sum(p, axis=0, keepdims=True)
                p_scr[h, u] = p.astype(bf16)
            out.append((m_new, l_new, alpha * (acc + pv[h])))
        for h, u in units:
            stash_scores(h, u, base + MOBA_UNROLL + u)
        return tuple(out)

    n_steps = (cur + MOBA_UNROLL - 1) // MOBA_UNROLL
    state = lax.fori_loop(0, n_steps, step, tuple(init))
    for h in heads:
        _, l_fin, acc = state[h]
        acc = acc + pending_pv(h, (n_steps - 1) * MOBA_UNROLL)
        z = z_ref[:, hcols[h]].astype(f32)
        o_ref[:, hcols[h]] = ((acc / l_fin).T * (z * jax.nn.sigmoid(z))).astype(bf16)


def _moba_attention(q, z, k, v_t, k_mean, bias, bsz, seq, name):
    m = bsz * seq
    nq = seq // MOBA_BLOCK
    hp = MOBA_HEADS_PER_STEP
    w = hp * HEAD_DIM
    return pl.pallas_call(
        _moba_kernel,
        grid=(bsz, HEADS // hp, nq),
        in_specs=[
            pl.BlockSpec((MOBA_BLOCK, w), lambda b, g, i: (b * nq + i, g)),
            pl.BlockSpec((MOBA_BLOCK, w), lambda b, g, i: (b * nq + i, g)),
            pl.BlockSpec((seq, w), lambda b, g, i: (b, g)),
            pl.BlockSpec((1, w, seq), lambda b, g, i: (b, g, 0)),
            pl.BlockSpec((1, nq, w), lambda b, g, i: (b, 0, g)),
            pl.BlockSpec((hp, N_BIAS_TILES, MOBA_BLOCK, MOBA_BLOCK), lambda b, g, i: (g, 0, 0, 0)),
        ],
        out_specs=pl.BlockSpec((MOBA_BLOCK, w), lambda b, g, i: (b * nq + i, g)),
        out_shape=jax.ShapeDtypeStruct((m, KEY_W), bf16),
        scratch_shapes=[pltpu.VMEM((hp, MOBA_UNROLL, MOBA_BLOCK, MOBA_BLOCK), f32),
                        pltpu.VMEM((hp, MOBA_UNROLL, 1, MOBA_BLOCK), f32),
                        pltpu.VMEM((hp, MOBA_UNROLL, MOBA_BLOCK, MOBA_BLOCK), bf16)],
        compiler_params=_cparams(("parallel", "parallel", "arbitrary")),
        name=name,
    )(q, z, k, v_t, k_mean, bias)


def kernel(x, a_norm_g, a_w_in, a_conv_w, a_log, a_dt_bias, a_out_norm_g, a_w_out, kv_norm_g, w_kv, b_norm_g, b_w_in, b_w_out, rel_bias, final_norm_g):
    bsz, seq, d = x.shape
    m = bsz * seq
    assert d == D_MODEL and seq % MOBA_BLOCK == 0 and seq % GDN_TB == 0 and m % MM_TM == 0 and m % GATES_TM == 0
    xf = x.reshape(m, d).astype(f32)

    qkvz_w = CONV_CH + KEY_W
    for i in range(a_w_in.shape[0]):
        w_ab = jnp.pad(a_w_in[i, :, qkvz_w:], ((0, 0), (0, LANES - 2 * HEADS)))
        proj, ab = _norm_matmul(xf, a_norm_g[i], a_w_in, i, qkvz_w, GDN_TN, bf16,
                                f"gdn{i}_in_proj", w_side=w_ab)
        cols = _gdn_gates(ab[0], a_log[i], a_dt_bias[i])
        o = _gdn_core(proj, a_conv_w[i].astype(f32), cols, a_out_norm_g[i].astype(f32),
                      bsz, seq, f"gdn{i}_core")
        xf = _out_proj(o, a_w_out, i, xf, final_norm_g, False, f"gdn{i}_out_proj")

    k, v_t, k_mean = _kv_proj(xf, kv_norm_g, w_kv, bsz, seq)
    k_mean = k_mean.reshape(bsz, seq // MOBA_BLOCK, KEY_W)
    bias = _bias_tiles(rel_bias)

    n_b = b_w_in.shape[0]
    for j in range(n_b):
        q = _norm_matmul(xf, b_norm_g[j], b_w_in, j, KEY_W, MOBA_IN_TN, f32, f"moba{j}_q_proj")
        z = _norm_matmul(xf, b_norm_g[j], b_w_in, j, KEY_W, MOBA_IN_TN, bf16, f"moba{j}_z_proj", col0=1)
        o = _moba_attention(q, z, k, v_t, k_mean, bias, bsz, seq, f"moba{j}_attn")
        xf = _out_proj(o, b_w_out, j, xf, final_norm_g, j == n_b - 1, f"moba{j}_out_proj")
    return xf.reshape(bsz, seq, d).astype(x.dtype)
```

```python
import functools
import math

import jax
import jax.numpy as jnp
from jax import lax
from jax.experimental import pallas as pl
from jax.experimental.pallas import tpu as pltpu

f32 = jnp.float32
bf16 = jnp.bfloat16

D_MODEL = 1024
HEADS = 8
HEAD_DIM = 128
KEY_W = HEADS * HEAD_DIM
CONV_CH = 3 * KEY_W
CONV_W = 4
GDN_IN_W = CONV_CH + KEY_W + 2 * HEADS
CHUNK = 64
MOBA_BLOCK = 256
MOBA_TOPK = 3
N_BUCKETS = 32
MAX_DIST = 2048
EPS = 1e-6
NEG_INF = float("-inf")
LOG2E = math.log2(math.e)

LANES = 128
SUBLANES = 8
VMEM_LIMIT_BYTES = 56 * 1024 * 1024

MM_TM = 1024
GDN_TN = 2048
MOBA_IN_TN = 1024
GATES_TM = 2048
GDN_TB = 256
GDN_GROUP = 4
MOBA_UNROLL = 4
MOBA_HEADS_PER_STEP = 2
N_BIAS_TILES = MAX_DIST // MOBA_BLOCK + 2


def _cparams(sem):
    return pltpu.CompilerParams(dimension_semantics=sem, vmem_limit_bytes=VMEM_LIMIT_BYTES)


def _dot(a, b):
    return jnp.dot(a, b, preferred_element_type=f32)


def _dot_nt(a, b):
    return lax.dot_general(a, b, (((1,), (1,)), ((), ())), preferred_element_type=f32)


def _dot_tn(a, b):
    return lax.dot_general(a, b, (((0,), (0,)), ((), ())), preferred_element_type=f32)


def _split(x):
    hi = x.astype(bf16)
    lo = (x - hi.astype(f32)).astype(bf16)
    return hi, lo


def _rmsnorm_bf16(x_ref, g_ref):
    x = x_ref[...]
    y = x * lax.rsqrt(jnp.mean(x * x, axis=-1, keepdims=True) + EPS)
    return (y * g_ref[...]).astype(bf16)


def _cast_weights_once(step, w_ref, wb_ref):
    @pl.when(step == 0)
    def _():
        wb_ref[...] = w_ref[...].astype(bf16)


def _norm_mm_kernel(x_ref, g_ref, w_ref, o_ref, wb_ref):
    _cast_weights_once(pl.program_id(1), w_ref, wb_ref)
    o_ref[...] = _dot(_rmsnorm_bf16(x_ref, g_ref), wb_ref[...]).astype(o_ref.dtype)


def _norm_mm_side_kernel(x_ref, g_ref, w_ref, ws_ref, o_ref, os_ref, wb_ref):
    _cast_weights_once(pl.program_id(1), w_ref, wb_ref)
    xn = _rmsnorm_bf16(x_ref, g_ref)
    o_ref[...] = _dot(xn, wb_ref[...]).astype(o_ref.dtype)
    os_ref[...] = _dot(xn, ws_ref[...].astype(bf16))


def _norm_matmul(x, g, w_stack, layer, n, tn, out_dtype, name, col0=0, w_side=None):
    m, k = x.shape
    in_specs = [
        pl.BlockSpec((MM_TM, k), lambda j, i: (i, 0)),
        pl.BlockSpec((1, k), lambda j, i: (0, 0)),
        pl.BlockSpec((None, k, tn), lambda j, i: (layer, 0, col0 + j)),
    ]
    out_specs = pl.BlockSpec((MM_TM, tn), lambda j, i: (i, j))
    out_shape = jax.ShapeDtypeStruct((m, n), out_dtype)
    args = [x, g.reshape(1, k).astype(f32), w_stack]
    body = _norm_mm_kernel
    if w_side is not None:
        ns = w_side.shape[1]
        in_specs.append(pl.BlockSpec((k, ns), lambda j, i: (0, 0)))
        out_specs = [out_specs, pl.BlockSpec((None, MM_TM, ns), lambda j, i: (j, i, 0))]
        out_shape = [out_shape, jax.ShapeDtypeStruct((n // tn, m, ns), f32)]
        args.append(w_side)
        body = _norm_mm_side_kernel
    return pl.pallas_call(
        body,
        grid=(n // tn, m // MM_TM),
        in_specs=in_specs,
        out_specs=out_specs,
        out_shape=out_shape,
        scratch_shapes=[pltpu.VMEM((k, tn), bf16)],
        compiler_params=_cparams(("arbitrary", "arbitrary")),
        name=name,
    )(*args)


def _kv_kernel(x_ref, g_ref, w_ref, k_ref, vt_ref, km_ref, wb_ref):
    _cast_weights_once(pl.program_id(0), w_ref, wb_ref)
    acc = _dot(_rmsnorm_bf16(x_ref, g_ref), wb_ref[...])
    k = acc[:, :KEY_W]
    k_ref[...] = k.astype(bf16)
    km_ref[0] = jnp.mean(k, axis=0, keepdims=True)
    vt_ref[0] = acc[:, KEY_W:].T.astype(bf16)


def _kv_proj(x, g, w, bsz, seq):
    m, k = x.shape
    nblk = seq // MOBA_BLOCK
    return pl.pallas_call(
        _kv_kernel,
        grid=(m // MOBA_BLOCK,),
        in_specs=[
            pl.BlockSpec((MOBA_BLOCK, k), lambda i: (i, 0)),
            pl.BlockSpec((1, k), lambda i: (0, 0)),
            pl.BlockSpec((k, 2 * KEY_W), lambda i: (0, 0)),
        ],
        out_specs=[
            pl.BlockSpec((MOBA_BLOCK, KEY_W), lambda i: (i, 0)),
            pl.BlockSpec((1, KEY_W, MOBA_BLOCK), lambda i: (i // nblk, 0, i % nblk)),
            pl.BlockSpec((1, 1, KEY_W), lambda i: (i, 0, 0)),
        ],
        out_shape=[
            jax.ShapeDtypeStruct((m, KEY_W), bf16),
            jax.ShapeDtypeStruct((bsz, KEY_W, seq), bf16),
            jax.ShapeDtypeStruct((m // MOBA_BLOCK, 1, KEY_W), f32),
        ],
        scratch_shapes=[pltpu.VMEM((k, 2 * KEY_W), bf16)],
        compiler_params=_cparams(("arbitrary",)),
        name="kv_proj",
    )(x, g.reshape(1, k).astype(f32), w)


def _out_kernel(a_ref, w_ref, r_ref, g_ref, o_ref, wb_ref, *, final_norm):
    _cast_weights_once(pl.program_id(0), w_ref, wb_ref)
    y = r_ref[...] + _dot(a_ref[...], wb_ref[...])
    if final_norm:
        y = y * lax.rsqrt(jnp.mean(y * y, axis=-1, keepdims=True) + EPS) * g_ref[...]
    o_ref[...] = y


def _out_proj(a, w_stack, layer, res, g, final_norm, name):
    m, k = a.shape
    n = w_stack.shape[2]
    return pl.pallas_call(
        functools.partial(_out_kernel, final_norm=final_norm),
        grid=(m // MM_TM,),
        in_specs=[
            pl.BlockSpec((MM_TM, k), lambda i: (i, 0)),
            pl.BlockSpec((None, k, n), lambda i: (layer, 0, 0)),
            pl.BlockSpec((MM_TM, n), lambda i: (i, 0)),
            pl.BlockSpec((1, n), lambda i: (0, 0)),
        ],
        out_specs=pl.BlockSpec((MM_TM, n), lambda i: (i, 0)),
        out_shape=jax.ShapeDtypeStruct((m, n), f32),
        scratch_shapes=[pltpu.VMEM((k, n), bf16)],
        compiler_params=_cparams(("arbitrary",)),
        name=name,
    )(a, w_stack, res, g.reshape(1, n).astype(f32))


def _gates_kernel(ab_ref, alog_ref, dt_ref, o_ref):
    ab = ab_ref[...]
    rows = ab.shape[0]
    x = ab + dt_ref[...]
    softplus = jnp.maximum(x, 0.0) + jnp.log1p(jnp.exp(-jnp.abs(x)))
    g = -jnp.exp(alog_ref[...]) * softplus
    pos = lax.broadcasted_iota(jnp.int32, ab.shape, 0) % CHUNK
    fwd = g
    bwd = g
    s = 1
    while s < CHUNK:
        fwd = fwd + jnp.where(pos >= s, pltpu.roll(fwd, s, 0), 0.0)
        bwd = bwd + jnp.where(pos < CHUNK - s, pltpu.roll(bwd, rows - s, 0), 0.0)
        s *= 2
    lane = lax.broadcasted_iota(jnp.int32, ab.shape, 1)
    out = jnp.where(lane < HEADS, fwd, jax.nn.sigmoid(ab))
    out = jnp.where(lane < 2 * HEADS, out, pltpu.roll(jnp.exp(fwd), 2 * HEADS, 1))
    out = jnp.where(lane < 3 * HEADS, out, pltpu.roll(jnp.exp(bwd - g), 3 * HEADS, 1))
    o_ref[...] = jnp.where(lane < 4 * HEADS, out, 0.0)


def _gdn_gates(ab, a_log, dt_bias):
    m = ab.shape[0]
    lane_pad = (0, LANES - HEADS)
    alog = jnp.pad(a_log.astype(f32), lane_pad).reshape(1, LANES)
    dtb = jnp.pad(dt_bias.astype(f32), lane_pad).reshape(1, LANES)
    tile = pl.BlockSpec((GATES_TM, LANES), lambda i: (i, 0))
    vec = pl.BlockSpec((1, LANES), lambda i: (0, 0))
    return pl.pallas_call(
        _gates_kernel,
        grid=(m // GATES_TM,),
        in_specs=[tile, vec, vec],
        out_specs=tile,
        out_shape=jax.ShapeDtypeStruct((m, LANES), f32),
        compiler_params=_cparams(("parallel",)),
        name="gdn_gates",
    )(ab, alog, dtb)


def _neumann_inverse(mats):
    n = mats[0].shape[0]
    row = lax.broadcasted_iota(jnp.int32, (n, n), 0)
    col = lax.broadcasted_iota(jnp.int32, (n, n), 1)
    eye = jnp.where(row == col, 1.0, 0.0).astype(f32)

    ts = [eye - a for a in mats]
    xbs = [a.astype(bf16) for a in mats]
    p = 1
    while True:
        xbs = [_dot(xb, xb).astype(bf16) for xb in xbs]
        p *= 2
        ts = [t + _dot(t.astype(bf16), xb) for t, xb in zip(ts, xbs)]
        if 2 * p >= n:
            return ts


def _gdn_kernel(x_ref, z_ref, cw_ref, cols_ref, og_ref, o_ref,
                stage, qkvn, state):
    tb = x_ref.shape[0]
    t = pl.program_id(1)

    @pl.when(t == 0)
    def _():
        stage[0:SUBLANES, :] = jnp.zeros((SUBLANES, CONV_CH), f32)
        state[...] = jnp.zeros_like(state)

    stage[SUBLANES:SUBLANES + tb, :] = x_ref[...].astype(f32)
    for cb in range(CONV_CH // LANES):
        cs = slice(cb * LANES, (cb + 1) * LANES)
        acc = stage[SUBLANES:SUBLANES + tb, cs] * cw_ref[CONV_W - 1:CONV_W, cs]
        for j in range(CONV_W - 1):
            lo = SUBLANES - (CONV_W - 1) + j
            acc = acc + stage[lo:lo + tb, cs] * cw_ref[j:j + 1, cs]
        y = acc * jax.nn.sigmoid(acc)
        if cb < 2 * HEADS:
            y = y * lax.rsqrt(jnp.sum(y * y, axis=-1, keepdims=True) + EPS)
            if cb < HEADS:
                y = y * (HEAD_DIM ** -0.5)
        qkvn[:, cs] = y
    stage[0:SUBLANES, :] = stage[tb:tb + SUBLANES, :]

    row = lax.broadcasted_iota(jnp.int32, (CHUNK, CHUNK), 0)
    col = lax.broadcasted_iota(jnp.int32, (CHUNK, CHUNK), 1)
    tril = row >= col
    strict = row > col
    og = og_ref[...]

    hs = range(HEADS)
    qcols = [slice(h * HEAD_DIM, (h + 1) * HEAD_DIM) for h in hs]

    def group_step(gi, carry):
        items = [(ci, h) for ci in range(GDN_GROUP) for h in hs]
        rows, g_rows, ctile = [], [], []
        for ci in range(GDN_GROUP):
            c = gi * GDN_GROUP + ci
            rows.append(pl.ds(pl.multiple_of(c * CHUNK, CHUNK), CHUNK))
            ctile.append(cols_ref[0, c])
            g_rows.append(ctile[ci].T)

        def col(ci, h, which):
            return ctile[ci][:, which * HEADS + h:which * HEADS + h + 1]

        q = {it: qkvn[rows[it[0]], qcols[it[1]]] for it in items}
        k = {(ci, h): qkvn[rows[ci], KEY_W + h * HEAD_DIM:KEY_W + (h + 1) * HEAD_DIM] for ci, h in items}
        v = {(ci, h): qkvn[rows[ci], 2 * KEY_W + h * HEAD_DIM:2 * KEY_W + (h + 1) * HEAD_DIM]
             for ci, h in items}
        decay = {(ci, h): jnp.exp(jnp.where(tril, col(ci, h, 0) - g_rows[ci][h:h + 1, :], NEG_INF))
                 for ci, h in items}
        k_beta = {it: k[it] * col(*it, 1) for it in items}
        kq = {it: _dot_nt(jnp.concatenate([k_beta[it], q[it]], axis=0).astype(bf16), k[it].astype(bf16))
              for it in items}
        t_inv = dict(zip(items, _neumann_inverse(
            [jnp.where(strict, kq[it][:CHUNK] * decay[it], 0.0) for it in items])))
        uw = {it: _dot(t_inv[it].astype(bf16),
                       jnp.concatenate([v[it] * col(*it, 1), k_beta[it] * col(*it, 2)], axis=1).astype(bf16))
              for it in items}
        w_qd = {it: jnp.concatenate([uw[it][:, HEAD_DIM:], q[it] * col(*it, 2)], axis=0).astype(bf16)
                for it in items}
        qk = {it: (kq[it][CHUNK:] * decay[it]).astype(bf16) for it in items}
        k_dec = {it: (k[it] * col(*it, 3)).astype(bf16) for it in items}

        for ci in range(GDN_GROUP):
            s_prev = [state[h] for h in hs]
            ws_qs = [_dot(w_qd[ci, h], s_prev[h].astype(bf16)) for h in hs]
            v_new = [(uw[ci, h][:, :HEAD_DIM] - ws_qs[h][:CHUNK]).astype(bf16) for h in hs]
            o = [ws_qs[h][CHUNK:] + _dot(qk[ci, h], v_new[h]) for h in hs]
            for h in hs:
                g_last = col(ci, h, 2)[CHUNK - 1:CHUNK, :]
                state[h] = s_prev[h] * g_last + _dot_tn(k_dec[ci, h], v_new[h])
            for h in hs:
                on = o[h] * lax.rsqrt(jnp.mean(o[h] * o[h], axis=-1, keepdims=True) + EPS) * og
                z = z_ref[rows[ci], qcols[h]].astype(f32)
                o_ref[rows[ci], qcols[h]] = (on * (z * jax.nn.sigmoid(z))).astype(bf16)
        return carry

    lax.fori_loop(0, tb // (CHUNK * GDN_GROUP), group_step, 0)


def _gdn_core(proj, conv_w, cols, out_norm_g, bsz, seq, name):
    m = bsz * seq
    nt = seq // GDN_TB
    cpt = GDN_TB // CHUNK
    return pl.pallas_call(
        _gdn_kernel,
        grid=(bsz, nt),
        in_specs=[
            pl.BlockSpec((GDN_TB, CONV_CH), lambda b, t: (b * nt + t, 0)),
            pl.BlockSpec((GDN_TB, KEY_W), lambda b, t: (b * nt + t, CONV_CH // KEY_W)),
            pl.BlockSpec((CONV_W, CONV_CH), lambda b, t: (0, 0)),
            pl.BlockSpec((1, cpt, CHUNK, LANES), lambda b, t: (b, t, 0, 0)),
            pl.BlockSpec((1, HEAD_DIM), lambda b, t: (0, 0)),
        ],
        out_specs=pl.BlockSpec((GDN_TB, KEY_W), lambda b, t: (b * nt + t, 0)),
        out_shape=jax.ShapeDtypeStruct((m, KEY_W), bf16),
        scratch_shapes=[
            pltpu.VMEM((GDN_TB + SUBLANES, CONV_CH), f32),
            pltpu.VMEM((GDN_TB, CONV_CH), f32),
            pltpu.VMEM((HEADS, HEAD_DIM, HEAD_DIM), f32),
        ],
        compiler_params=_cparams(("parallel", "arbitrary")),
        name=name,
    )(proj, proj, conv_w, cols.reshape(bsz, seq // CHUNK, CHUNK, LANES), out_norm_g.reshape(1, HEAD_DIM))


def _bias_kernel(rb_ref, o_ref):
    h = pl.program_id(0)
    d = pl.program_id(1)
    key = lax.broadcasted_iota(jnp.int32, (MOBA_BLOCK, MOBA_BLOCK), 0)
    qry = lax.broadcasted_iota(jnp.int32, (MOBA_BLOCK, MOBA_BLOCK), 1)
    n = jnp.maximum(d * MOBA_BLOCK + qry - key, 0)
    max_exact = N_BUCKETS // 2
    nf = jnp.maximum(n, 1).astype(f32)
    large = max_exact + (jnp.log(nf / max_exact) / math.log(MAX_DIST / max_exact)
                         * (N_BUCKETS - max_exact)).astype(jnp.int32)
    large = jnp.minimum(large, N_BUCKETS - 1)
    bucket = jnp.where(n < max_exact, n, large)
    out = jnp.zeros((MOBA_BLOCK, MOBA_BLOCK), f32)
    for b in range(N_BUCKETS):
        out = jnp.where(bucket == b, rb_ref[h, b], out)
    o_ref[0, 0] = out * LOG2E


def _bias_tiles(rel_bias):
    return pl.pallas_call(
        _bias_kernel,
        grid=(HEADS, N_BIAS_TILES),
        in_specs=[pl.BlockSpec(memory_space=pltpu.SMEM)],
        out_specs=pl.BlockSpec((1, 1, MOBA_BLOCK, MOBA_BLOCK), lambda h, d: (h, d, 0, 0)),
        out_shape=jax.ShapeDtypeStruct((HEADS, N_BIAS_TILES, MOBA_BLOCK, MOBA_BLOCK), f32),
        compiler_params=_cparams(("parallel", "parallel")),
        name="t5_bias_tiles",
    )(rel_bias.T.astype(f32))


def _moba_kernel(q_ref, z_ref, k_ref, vt_ref, km_ref, bias_ref, o_ref, s_scr, smax_scr, p_scr):
    cur = pl.program_id(2)
    nblk = km_ref.shape[1]
    heads = range(MOBA_HEADS_PER_STEP)
    hcols = [slice(h * HEAD_DIM, (h + 1) * HEAD_DIM) for h in heads]
    q_t = [q_ref[:, cs].T for cs in hcols]

    blk = lax.broadcasted_iota(jnp.int32, (nblk, MOBA_BLOCK), 0).astype(f32)
    gates = []
    for h in heads:
        qh, ql = _split(q_t[h])
        kmh, kml = _split(km_ref[0, :, hcols[h]])
        gate = _dot(kmh, qh) + _dot(kml, qh) + _dot(kmh, ql)
        gates.append(jnp.where(blk < cur.astype(f32), gate, NEG_INF))
    sels = [[] for _ in heads]
    for _ in range(MOBA_TOPK):
        for h in heads:
            best = jnp.max(gates[h], axis=0, keepdims=True)
            idx = jnp.min(jnp.where(gates[h] == best, blk, float(nblk)), axis=0, keepdims=True)
            idx = jnp.where(best > NEG_INF, idx, -1.0)
            sels[h].append(idx)
            gates[h] = jnp.where(blk == idx, NEG_INF, gates[h])

    qb = [(q_t[h] * (HEAD_DIM ** -0.5 * LOG2E)).astype(bf16) for h in heads]

    def block_scores(h, j):
        jc = jnp.minimum(j, nblk - 1)
        dist = jnp.clip(cur - jc, 0, N_BIAS_TILES - 1)
        kj = k_ref[pl.ds(pl.multiple_of(jc * MOBA_BLOCK, MOBA_BLOCK), MOBA_BLOCK), hcols[h]]
        return _dot(kj, qb[h]) + bias_ref[h, dist]

    def block_values(h, j):
        jc = jnp.minimum(j, nblk - 1)
        return vt_ref[0, hcols[h], pl.ds(pl.multiple_of(jc * MOBA_BLOCK, MOBA_BLOCK), MOBA_BLOCK)]

    key = lax.broadcasted_iota(jnp.int32, (MOBA_BLOCK, MOBA_BLOCK), 0)
    qry = lax.broadcasted_iota(jnp.int32, (MOBA_BLOCK, MOBA_BLOCK), 1)
    init = []
    for h in heads:
        s = jnp.where(qry >= key, block_scores(h, cur), NEG_INF)
        m0 = jnp.max(s, axis=0, keepdims=True)
        p = jnp.exp2(s - m0)
        l0 = jnp.sum(p, axis=0, keepdims=True)
        acc0 = _dot(block_values(h, cur), p.astype(bf16))
        init.append((m0, l0, acc0))

    units = [(h, u) for h in heads for u in range(MOBA_UNROLL)]

    def stash_scores(h, u, j):
        s = block_scores(h, j)
        s_scr[h, u] = s
        smax_scr[h, u] = jnp.max(s, axis=0, keepdims=True)

    def pending_pv(h, first_block):
        pv = None
        for u in range(MOBA_UNROLL):
            t = _dot(block_values(h, jnp.maximum(first_block + u, 0)), p_scr[h, u])
            pv = t if pv is None else pv + t
        return pv

    for h, u in units:
        stash_scores(h, u, jnp.int32(u))
        p_scr[h, u] = jnp.zeros((MOBA_BLOCK, MOBA_BLOCK), bf16)

    def softmax_group(h, base, m_prev, l_prev):
        chosen = []
        for u in range(MOBA_UNROLL):
            j = base + u
            jf = j.astype(f32)
            chosen.append(((sels[h][0] == jf) | (sels[h][1] == jf) | (sels[h][2] == jf)) & (j < cur))
        m_new = m_prev
        for u in range(MOBA_UNROLL):
            m_new = jnp.maximum(m_new, jnp.where(chosen[u], smax_scr[h, u], NEG_INF))
        alpha = jnp.exp2(m_prev - m_new)
        l_new = alpha * l_prev
        ps = []
        for u in range(MOBA_UNROLL):
            p = jnp.exp2(s_scr[h, u] - jnp.where(chosen[u], m_new, float("inf")))
            l_new = l_new + jnp.sum(p, axis=0, keepdims=True)
            ps.append(p.astype(bf16))
        return m_new, l_new, alpha, ps

    def step(it, state):
        base = it * MOBA_UNROLL
        pv = [pending_pv(h, base - MOBA_UNROLL) for h in heads]
        out = []
        for h in heads:
            m_prev, l_prev, acc = state[h]
            m_new, l_new, alpha, ps = softmax_group(h, base, m_prev, l_prev)
            for u in range(MOBA_UNROLL):
                p_scr[h, u] = ps[u]
            out.append((m_new, l_new, alpha * (acc + pv[h])))
        for h, u in units:
            stash_scores(h, u, base + MOBA_UNROLL + u)
        return tuple(out)

    def finish(h, l_fin, acc):
        z = z_ref[:, hcols[h]].astype(f32)
        o_ref[:, hcols[h]] = ((acc / l_fin).T * (z * jax.nn.sigmoid(z))).astype(bf16)

    n_steps = (cur + MOBA_UNROLL - 1) // MOBA_UNROLL
    state = lax.fori_loop(0, jnp.maximum(n_steps - 1, 0), step, tuple(init))

    @pl.when(n_steps >= 1)
    def _():
        base = (n_steps - 1) * MOBA_UNROLL
        for h in heads:
            m_prev, l_prev, acc = state[h]
            pv = pending_pv(h, base - MOBA_UNROLL)
            _, l_fin, alpha, ps = softmax_group(h, base, m_prev, l_prev)
            acc = alpha * (acc + pv)
            for u in range(MOBA_UNROLL):
                acc = acc + _dot(block_values(h, base + u), ps[u])
            finish(h, l_fin, acc)

    @pl.when(n_steps == 0)
    def _():
        for h in heads:
            finish(h, state[h][1], state[h][2])


def _moba_attention(q, z, k, v_t, k_mean, bias, bsz, seq, name):
    m = bsz * seq
    nq = seq // MOBA_BLOCK
    hp = MOBA_HEADS_PER_STEP
    w = hp * HEAD_DIM
    return pl.pallas_call(
        _moba_kernel,
        grid=(bsz, HEADS // hp, nq),
        in_specs=[
            pl.BlockSpec((MOBA_BLOCK, w), lambda b, g, i: (b * nq + i, g)),
            pl.BlockSpec((MOBA_BLOCK, w), lambda b, g, i: (b * nq + i, g)),
            pl.BlockSpec((seq, w), lambda b, g, i: (b, g)),
            pl.BlockSpec((1, w, seq), lambda b, g, i: (b, g, 0)),
            pl.BlockSpec((1, nq, w), lambda b, g, i: (b, 0, g)),
            pl.BlockSpec((hp, N_BIAS_TILES, MOBA_BLOCK, MOBA_BLOCK), lambda b, g, i: (g, 0, 0, 0)),
        ],
        out_specs=pl.BlockSpec((MOBA_BLOCK, w), lambda b, g, i: (b * nq + i, g)),
        out_shape=jax.ShapeDtypeStruct((m, KEY_W), bf16),
        scratch_shapes=[pltpu.VMEM((hp, MOBA_UNROLL, MOBA_BLOCK, MOBA_BLOCK), f32),
                        pltpu.VMEM((hp, MOBA_UNROLL, 1, MOBA_BLOCK), f32),
                        pltpu.VMEM((hp, MOBA_UNROLL, MOBA_BLOCK, MOBA_BLOCK), bf16)],
        compiler_params=_cparams(("parallel", "parallel", "arbitrary")),
        name=name,
    )(q, z, k, v_t, k_mean, bias)


def kernel(x, a_norm_g, a_w_in, a_conv_w, a_log, a_dt_bias, a_out_norm_g, a_w_out, kv_norm_g, w_kv, b_norm_g, b_w_in, b_w_out, rel_bias, final_norm_g):
    bsz, seq, d = x.shape
    m = bsz * seq
    assert d == D_MODEL and seq % MOBA_BLOCK == 0 and seq % GDN_TB == 0 and m % MM_TM == 0 and m % GATES_TM == 0
    xf = x.reshape(m, d).astype(f32)

    qkvz_w = CONV_CH + KEY_W
    for i in range(a_w_in.shape[0]):
        w_ab = jnp.pad(a_w_in[i, :, qkvz_w:], ((0, 0), (0, LANES - 2 * HEADS)))
        proj, ab = _norm_matmul(xf, a_norm_g[i], a_w_in, i, qkvz_w, GDN_TN, bf16,
                                f"gdn{i}_in_proj", w_side=w_ab)
        cols = _gdn_gates(ab[0], a_log[i], a_dt_bias[i])
        o = _gdn_core(proj, a_conv_w[i].astype(f32), cols, a_out_norm_g[i].astype(f32),
                      bsz, seq, f"gdn{i}_core")
        xf = _out_proj(o, a_w_out, i, xf, final_norm_g, False, f"gdn{i}_out_proj")

    k, v_t, k_mean = _kv_proj(xf, kv_norm_g, w_kv, bsz, seq)
    k_mean = k_mean.reshape(bsz, seq // MOBA_BLOCK, KEY_W)
    bias = _bias_tiles(rel_bias)

    n_b = b_w_in.shape[0]
    for j in range(n_b):
        q = _norm_matmul(xf, b_norm_g[j], b_w_in, j, KEY_W, MOBA_IN_TN, f32, f"moba{j}_q_proj")
        z = _norm_matmul(xf, b_norm_g[j], b_w_in, j, KEY_W, MOBA_IN_TN, bf16, f"moba{j}_z_proj", col0=1)
        o = _moba_attention(q, z, k, v_t, k_mean, bias, bsz, seq, f"moba{j}_attn")
        xf = _out_proj(o, b_w_out, j, xf, final_norm_g, j == n_b - 1, f"moba{j}_out_proj")
    return xf.reshape(bsz, seq, d).astype(x.dtype)
```

```python
import functools
import math

import jax
import jax.numpy as jnp
from jax import lax
from jax.experimental import pallas as pl
from jax.experimental.pallas import tpu as pltpu

f32 = jnp.float32
bf16 = jnp.bfloat16

D_MODEL = 1024
HEADS = 8
HEAD_DIM = 128
KEY_W = HEADS * HEAD_DIM
CONV_CH = 3 * KEY_W
CONV_W = 4
GDN_IN_W = CONV_CH + KEY_W + 2 * HEADS
CHUNK = 64
MOBA_BLOCK = 256
MOBA_TOPK = 3
N_BUCKETS = 32
MAX_DIST = 2048
EPS = 1e-6
NEG_INF = float("-inf")
LOG2E = math.log2(math.e)

LANES = 128
SUBLANES = 8
VMEM_LIMIT_BYTES = 56 * 1024 * 1024

MM_TM = 1024
GDN_TN = 2048
MOBA_IN_TM = 512
GATES_TM = 2048
GDN_TB = 256
GDN_GROUP = 4
CONV_BLOCK = 256
MOBA_UNROLL = 4
MOBA_HEADS_PER_STEP = 2
N_BIAS_TILES = MAX_DIST // MOBA_BLOCK + 2


def _cparams(sem):
    return pltpu.CompilerParams(dimension_semantics=sem, vmem_limit_bytes=VMEM_LIMIT_BYTES)


def _dot(a, b):
    return jnp.dot(a, b, preferred_element_type=f32)


def _dot_nt(a, b):
    return lax.dot_general(a, b, (((1,), (1,)), ((), ())), preferred_element_type=f32)


def _dot_tn(a, b):
    return lax.dot_general(a, b, (((0,), (0,)), ((), ())), preferred_element_type=f32)


def _split(x):
    hi = x.astype(bf16)
    lo = (x - hi.astype(f32)).astype(bf16)
    return hi, lo


def _rmsnorm_bf16(x_ref, g_ref):
    x = x_ref[...]
    y = x * lax.rsqrt(jnp.mean(x * x, axis=-1, keepdims=True) + EPS)
    return (y * g_ref[...]).astype(bf16)


def _cast_weights_once(step, w_ref, wb_ref):
    @pl.when(step == 0)
    def _():
        wb_ref[...] = w_ref[...].astype(bf16)


def _norm_mm_kernel(x_ref, g_ref, w_ref, o_ref, wb_ref):
    _cast_weights_once(pl.program_id(1), w_ref, wb_ref)
    o_ref[...] = _dot(_rmsnorm_bf16(x_ref, g_ref), wb_ref[...]).astype(o_ref.dtype)


def _norm_mm_side_kernel(x_ref, g_ref, w_ref, ws_ref, o_ref, os_ref, wb_ref):
    _cast_weights_once(pl.program_id(1), w_ref, wb_ref)
    xn = _rmsnorm_bf16(x_ref, g_ref)
    o_ref[...] = _dot(xn, wb_ref[...]).astype(o_ref.dtype)
    os_ref[...] = _dot(xn, ws_ref[...].astype(bf16))


def _norm_matmul(x, g, w_stack, layer, n, tn, out_dtype, name, w_side=None):
    m, k = x.shape
    in_specs = [
        pl.BlockSpec((MM_TM, k), lambda j, i: (i, 0)),
        pl.BlockSpec((1, k), lambda j, i: (0, 0)),
        pl.BlockSpec((None, k, tn), lambda j, i: (layer, 0, j)),
    ]
    out_specs = pl.BlockSpec((MM_TM, tn), lambda j, i: (i, j))
    out_shape = jax.ShapeDtypeStruct((m, n), out_dtype)
    args = [x, g.reshape(1, k).astype(f32), w_stack]
    body = _norm_mm_kernel
    if w_side is not None:
        ns = w_side.shape[1]
        in_specs.append(pl.BlockSpec((k, ns), lambda j, i: (0, 0)))
        out_specs = [out_specs, pl.BlockSpec((None, MM_TM, ns), lambda j, i: (j, i, 0))]
        out_shape = [out_shape, jax.ShapeDtypeStruct((n // tn, m, ns), f32)]
        args.append(w_side)
        body = _norm_mm_side_kernel
    return pl.pallas_call(
        body,
        grid=(n // tn, m // MM_TM),
        in_specs=in_specs,
        out_specs=out_specs,
        out_shape=out_shape,
        scratch_shapes=[pltpu.VMEM((k, tn), bf16)],
        compiler_params=_cparams(("arbitrary", "arbitrary")),
        name=name,
    )(*args)


def _moba_in_kernel(x_ref, g_ref, w_ref, q_ref, z_ref, wb_ref):
    _cast_weights_once(pl.program_id(0), w_ref, wb_ref)
    xn = _rmsnorm_bf16(x_ref, g_ref)
    q_ref[...] = _dot(xn, wb_ref[:, :KEY_W])
    z_ref[...] = _dot(xn, wb_ref[:, KEY_W:]).astype(bf16)


def _moba_in_proj(x, g, w_stack, layer, name):
    m, k = x.shape
    return pl.pallas_call(
        _moba_in_kernel,
        grid=(m // MOBA_IN_TM,),
        in_specs=[
            pl.BlockSpec((MOBA_IN_TM, k), lambda i: (i, 0)),
            pl.BlockSpec((1, k), lambda i: (0, 0)),
            pl.BlockSpec((None, k, 2 * KEY_W), lambda i: (layer, 0, 0)),
        ],
        out_specs=[
            pl.BlockSpec((MOBA_IN_TM, KEY_W), lambda i: (i, 0)),
            pl.BlockSpec((MOBA_IN_TM, KEY_W), lambda i: (i, 0)),
        ],
        out_shape=[jax.ShapeDtypeStruct((m, KEY_W), f32), jax.ShapeDtypeStruct((m, KEY_W), bf16)],
        scratch_shapes=[pltpu.VMEM((k, 2 * KEY_W), bf16)],
        compiler_params=_cparams(("arbitrary",)),
        name=name,
    )(x, g.reshape(1, k).astype(f32), w_stack)


def _kv_kernel(x_ref, g_ref, w_ref, k_ref, vt_ref, km_ref, wb_ref):
    _cast_weights_once(pl.program_id(0), w_ref, wb_ref)
    acc = _dot(_rmsnorm_bf16(x_ref, g_ref), wb_ref[...])
    k = acc[:, :KEY_W]
    k_ref[...] = k.astype(bf16)
    km_ref[0] = jnp.mean(k, axis=0, keepdims=True)
    vt_ref[0] = acc[:, KEY_W:].T.astype(bf16)


def _kv_proj(x, g, w, bsz, seq):
    m, k = x.shape
    nblk = seq // MOBA_BLOCK
    return pl.pallas_call(
        _kv_kernel,
        grid=(m // MOBA_BLOCK,),
        in_specs=[
            pl.BlockSpec((MOBA_BLOCK, k), lambda i: (i, 0)),
            pl.BlockSpec((1, k), lambda i: (0, 0)),
            pl.BlockSpec((k, 2 * KEY_W), lambda i: (0, 0)),
        ],
        out_specs=[
            pl.BlockSpec((MOBA_BLOCK, KEY_W), lambda i: (i, 0)),
            pl.BlockSpec((1, KEY_W, MOBA_BLOCK), lambda i: (i // nblk, 0, i % nblk)),
            pl.BlockSpec((1, 1, KEY_W), lambda i: (i, 0, 0)),
        ],
        out_shape=[
            jax.ShapeDtypeStruct((m, KEY_W), bf16),
            jax.ShapeDtypeStruct((bsz, KEY_W, seq), bf16),
            jax.ShapeDtypeStruct((m // MOBA_BLOCK, 1, KEY_W), f32),
        ],
        scratch_shapes=[pltpu.VMEM((k, 2 * KEY_W), bf16)],
        compiler_params=_cparams(("arbitrary",)),
        name="kv_proj",
    )(x, g.reshape(1, k).astype(f32), w)


def _out_kernel(a_ref, w_ref, r_ref, g_ref, o_ref, wb_ref, *, final_norm):
    _cast_weights_once(pl.program_id(0), w_ref, wb_ref)
    y = r_ref[...] + _dot(a_ref[...], wb_ref[...])
    if final_norm:
        y = y * lax.rsqrt(jnp.mean(y * y, axis=-1, keepdims=True) + EPS) * g_ref[...]
    o_ref[...] = y


def _out_proj(a, w_stack, layer, res, g, final_norm, name):
    m, k = a.shape
    n = w_stack.shape[2]
    return pl.pallas_call(
        functools.partial(_out_kernel, final_norm=final_norm),
        grid=(m // MM_TM,),
        in_specs=[
            pl.BlockSpec((MM_TM, k), lambda i: (i, 0)),
            pl.BlockSpec((None, k, n), lambda i: (layer, 0, 0)),
            pl.BlockSpec((MM_TM, n), lambda i: (i, 0)),
            pl.BlockSpec((1, n), lambda i: (0, 0)),
        ],
        out_specs=pl.BlockSpec((MM_TM, n), lambda i: (i, 0)),
        out_shape=jax.ShapeDtypeStruct((m, n), f32),
        scratch_shapes=[pltpu.VMEM((k, n), bf16)],
        compiler_params=_cparams(("arbitrary",)),
        name=name,
    )(a, w_stack, res, g.reshape(1, n).astype(f32))


def _gates_kernel(ab_ref, alog_ref, dt_ref, o_ref):
    ab = ab_ref[...]
    rows = ab.shape[0]
    x = ab + dt_ref[...]
    softplus = jnp.maximum(x, 0.0) + jnp.log1p(jnp.exp(-jnp.abs(x)))
    g = -jnp.exp(alog_ref[...]) * softplus
    pos = lax.broadcasted_iota(jnp.int32, ab.shape, 0) % CHUNK
    fwd = g
    bwd = g
    s = 1
    while s < CHUNK:
        fwd = fwd + jnp.where(pos >= s, pltpu.roll(fwd, s, 0), 0.0)
        bwd = bwd + jnp.where(pos < CHUNK - s, pltpu.roll(bwd, rows - s, 0), 0.0)
        s *= 2
    lane = lax.broadcasted_iota(jnp.int32, ab.shape, 1)
    out = jnp.where(lane < HEADS, fwd, jax.nn.sigmoid(ab))
    out = jnp.where(lane < 2 * HEADS, out, pltpu.roll(jnp.exp(fwd), 2 * HEADS, 1))
    out = jnp.where(lane < 3 * HEADS, out, pltpu.roll(jnp.exp(bwd - g), 3 * HEADS, 1))
    o_ref[...] = jnp.where(lane < 4 * HEADS, out, 0.0)


def _gdn_gates(ab, a_log, dt_bias):
    m = ab.shape[0]
    lane_pad = (0, LANES - HEADS)
    alog = jnp.pad(a_log.astype(f32), lane_pad).reshape(1, LANES)
    dtb = jnp.pad(dt_bias.astype(f32), lane_pad).reshape(1, LANES)
    tile = pl.BlockSpec((GATES_TM, LANES), lambda i: (i, 0))
    vec = pl.BlockSpec((1, LANES), lambda i: (0, 0))
    return pl.pallas_call(
        _gates_kernel,
        grid=(m // GATES_TM,),
        in_specs=[tile, vec, vec],
        out_specs=tile,
        out_shape=jax.ShapeDtypeStruct((m, LANES), f32),
        compiler_params=_cparams(("parallel",)),
        name="gdn_gates",
    )(ab, alog, dtb)


def _neumann_inverse(mats):
    n = mats[0].shape[0]
    row = lax.broadcasted_iota(jnp.int32, (n, n), 0)
    col = lax.broadcasted_iota(jnp.int32, (n, n), 1)
    eye = jnp.where(row == col, 1.0, 0.0).astype(f32)

    ts = [eye - a for a in mats]
    xbs = [a.astype(bf16) for a in mats]
    p = 1
    while True:
        xbs = [_dot(xb, xb).astype(bf16) for xb in xbs]
        p *= 2
        ts = [t + _dot(t.astype(bf16), xb) for t, xb in zip(ts, xbs)]
        if 2 * p >= n:
            return ts


def _gdn_kernel(x_ref, z_ref, cw_ref, cols_ref, og_ref, o_ref,
                halo, qkvn, state):
    tb = x_ref.shape[0]
    t = pl.program_id(1)

    @pl.when(t == 0)
    def _():
        halo[...] = jnp.zeros_like(halo)
        state[...] = jnp.zeros_like(state)

    trow = lax.broadcasted_iota(jnp.int32, (tb, tb), 0)
    tcol = lax.broadcasted_iota(jnp.int32, (tb, tb), 1)
    shifts = [jnp.where(trow - tcol == s, 1.0, 0.0).astype(bf16) for s in range(1, CONV_W)]
    hrow = lax.broadcasted_iota(jnp.int32, (SUBLANES, CONV_BLOCK), 0)
    for cb in range(CONV_CH // CONV_BLOCK):
        cs = slice(cb * CONV_BLOCK, (cb + 1) * CONV_BLOCK)
        xb = x_ref[:, cs]
        acc = xb.astype(f32) * cw_ref[CONV_W - 1:CONV_W, cs]
        patch = jnp.zeros((SUBLANES, CONV_BLOCK), f32)
        for s in range(1, CONV_W):
            w_s = cw_ref[CONV_W - 1 - s:CONV_W - s, cs]
            acc = acc + _dot(shifts[s - 1], xb) * w_s
            patch = patch + jnp.where(hrow < s, pltpu.roll(halo[:, cs], s, 0), 0.0) * w_s
        acc = jnp.concatenate([acc[:SUBLANES] + patch, acc[SUBLANES:]], axis=0)
        y = acc * jax.nn.sigmoid(acc)
        for half in range(CONV_BLOCK // LANES):
            lane_block = cb * (CONV_BLOCK // LANES) + half
            yh = y[:, half * LANES:(half + 1) * LANES]
            if lane_block < 2 * HEADS:
                yh = yh * lax.rsqrt(jnp.sum(yh * yh, axis=-1, keepdims=True) + EPS)
                if lane_block < HEADS:
                    yh = yh * (HEAD_DIM ** -0.5)
            qkvn[:, lane_block * LANES:(lane_block + 1) * LANES] = yh
    halo[...] = x_ref[tb - 2 * SUBLANES:tb, :].astype(f32)[SUBLANES:]

    row = lax.broadcasted_iota(jnp.int32, (CHUNK, CHUNK), 0)
    col = lax.broadcasted_iota(jnp.int32, (CHUNK, CHUNK), 1)
    tril = row >= col
    strict = row > col
    og = og_ref[...]

    hs = range(HEADS)
    qcols = [slice(h * HEAD_DIM, (h + 1) * HEAD_DIM) for h in hs]

    def group_step(gi, carry):
        items = [(ci, h) for ci in range(GDN_GROUP) for h in hs]
        rows, g_rows, ctile = [], [], []
        for ci in range(GDN_GROUP):
            c = gi * GDN_GROUP + ci
            rows.append(pl.ds(pl.multiple_of(c * CHUNK, CHUNK), CHUNK))
            ctile.append(cols_ref[0, c])
            g_rows.append(ctile[ci].T)

        def col(ci, h, which):
            return ctile[ci][:, which * HEADS + h:which * HEADS + h + 1]

        q = {it: qkvn[rows[it[0]], qcols[it[1]]] for it in items}
        k = {(ci, h): qkvn[rows[ci], KEY_W + h * HEAD_DIM:KEY_W + (h + 1) * HEAD_DIM] for ci, h in items}
        v = {(ci, h): qkvn[rows[ci], 2 * KEY_W + h * HEAD_DIM:2 * KEY_W + (h + 1) * HEAD_DIM]
             for ci, h in items}
        decay = {(ci, h): jnp.exp(jnp.where(tril, col(ci, h, 0) - g_rows[ci][h:h + 1, :], NEG_INF))
                 for ci, h in items}
        k_beta = {it: k[it] * col(*it, 1) for it in items}
        kq = {it: _dot_nt(jnp.concatenate([k_beta[it], q[it]], axis=0).astype(bf16), k[it].astype(bf16))
              for it in items}
        t_inv = dict(zip(items, _neumann_inverse(
            [jnp.where(strict, kq[it][:CHUNK] * decay[it], 0.0) for it in items])))
        uw = {it: _dot(t_inv[it].astype(bf16),
                       jnp.concatenate([v[it] * col(*it, 1), k_beta[it] * col(*it, 2)], axis=1).astype(bf16))
              for it in items}
        w_qd = {it: jnp.concatenate([uw[it][:, HEAD_DIM:], q[it] * col(*it, 2)], axis=0).astype(bf16)
                for it in items}
        qk = {it: (kq[it][CHUNK:] * decay[it]).astype(bf16) for it in items}
        k_dec = {it: (k[it] * col(*it, 3)).astype(bf16) for it in items}

        for ci in range(GDN_GROUP):
            s_prev = [state[h] for h in hs]
            ws_qs = [_dot(w_qd[ci, h], s_prev[h].astype(bf16)) for h in hs]
            v_new = [(uw[ci, h][:, :HEAD_DIM] - ws_qs[h][:CHUNK]).astype(bf16) for h in hs]
            o = [ws_qs[h][CHUNK:] + _dot(qk[ci, h], v_new[h]) for h in hs]
            for h in hs:
                g_last = col(ci, h, 2)[CHUNK - 1:CHUNK, :]
                state[h] = s_prev[h] * g_last + _dot_tn(k_dec[ci, h], v_new[h])
            for h in hs:
                on = o[h] * lax.rsqrt(jnp.mean(o[h] * o[h], axis=-1, keepdims=True) + EPS) * og
                z = z_ref[rows[ci], qcols[h]].astype(f32)
                o_ref[rows[ci], qcols[h]] = (on * (z * jax.nn.sigmoid(z))).astype(bf16)
        return carry

    lax.fori_loop(0, tb // (CHUNK * GDN_GROUP), group_step, 0)


def _gdn_core(proj, conv_w, cols, out_norm_g, bsz, seq, name):
    m = bsz * seq
    nt = seq // GDN_TB
    cpt = GDN_TB // CHUNK
    return pl.pallas_call(
        _gdn_kernel,
        grid=(bsz, nt),
        in_specs=[
            pl.BlockSpec((GDN_TB, CONV_CH), lambda b, t: (b * nt + t, 0)),
            pl.BlockSpec((GDN_TB, KEY_W), lambda b, t: (b * nt + t, CONV_CH // KEY_W)),
            pl.BlockSpec((CONV_W, CONV_CH), lambda b, t: (0, 0)),
            pl.BlockSpec((1, cpt, CHUNK, LANES), lambda b, t: (b, t, 0, 0)),
            pl.BlockSpec((1, HEAD_DIM), lambda b, t: (0, 0)),
        ],
        out_specs=pl.BlockSpec((GDN_TB, KEY_W), lambda b, t: (b * nt + t, 0)),
        out_shape=jax.ShapeDtypeStruct((m, KEY_W), bf16),
        scratch_shapes=[
            pltpu.VMEM((SUBLANES, CONV_CH), f32),
            pltpu.VMEM((GDN_TB, CONV_CH), f32),
            pltpu.VMEM((HEADS, HEAD_DIM, HEAD_DIM), f32),
        ],
        compiler_params=_cparams(("parallel", "arbitrary")),
        name=name,
    )(proj, proj, conv_w, cols.reshape(bsz, seq // CHUNK, CHUNK, LANES), out_norm_g.reshape(1, HEAD_DIM))


def _t5_bucket_host(n):
    max_exact = N_BUCKETS // 2
    if n < max_exact:
        return n
    large = max_exact + int(math.log(n / max_exact) / math.log(MAX_DIST / max_exact) * (N_BUCKETS - max_exact))
    return min(large, N_BUCKETS - 1)


def _bias_kernel(rb_ref, o_ref):
    h = pl.program_id(0)
    key = lax.broadcasted_iota(jnp.int32, (MOBA_BLOCK, MOBA_BLOCK), 0)
    qry = lax.broadcasted_iota(jnp.int32, (MOBA_BLOCK, MOBA_BLOCK), 1)
    max_exact = N_BUCKETS // 2
    for d in range(N_BIAS_TILES):
        n = jnp.maximum(d * MOBA_BLOCK + qry - key, 0)
        nf = jnp.maximum(n, 1).astype(f32)
        large = max_exact + (jnp.log(nf / max_exact) / math.log(MAX_DIST / max_exact)
                             * (N_BUCKETS - max_exact)).astype(jnp.int32)
        large = jnp.minimum(large, N_BUCKETS - 1)
        bucket = jnp.where(n < max_exact, n, large)
        b_lo = max(_t5_bucket_host(max(d * MOBA_BLOCK - (MOBA_BLOCK - 1), 0)) - 1, 0)
        b_hi = min(_t5_bucket_host(d * MOBA_BLOCK + MOBA_BLOCK - 1) + 1, N_BUCKETS - 1)
        out = jnp.zeros((MOBA_BLOCK, MOBA_BLOCK), f32)
        for b in range(b_lo, b_hi + 1):
            out = jnp.where(bucket == b, rb_ref[h, b], out)
        o_ref[0, d] = out * LOG2E


def _bias_tiles(rel_bias):
    return pl.pallas_call(
        _bias_kernel,
        grid=(HEADS,),
        in_specs=[pl.BlockSpec(memory_space=pltpu.SMEM)],
        out_specs=pl.BlockSpec((1, N_BIAS_TILES, MOBA_BLOCK, MOBA_BLOCK), lambda h: (h, 0, 0, 0)),
        out_shape=jax.ShapeDtypeStruct((HEADS, N_BIAS_TILES, MOBA_BLOCK, MOBA_BLOCK), f32),
        compiler_params=_cparams(("parallel",)),
        name="t5_bias_tiles",
    )(rel_bias.T.astype(f32))


def _moba_kernel(q_ref, z_ref, k_ref, vt_ref, km_ref, bias_ref, o_ref, s_scr, smax_scr, p_scr):
    cur = pl.program_id(2)
    nblk = km_ref.shape[1]
    heads = range(MOBA_HEADS_PER_STEP)
    hcols = [slice(h * HEAD_DIM, (h + 1) * HEAD_DIM) for h in heads]
    q_t = [q_ref[:, cs].T for cs in hcols]

    blk = lax.broadcasted_iota(jnp.int32, (nblk, MOBA_BLOCK), 0).astype(f32)
    gates = []
    for h in heads:
        qh, ql = _split(q_t[h])
        kmh, kml = _split(km_ref[0, :, hcols[h]])
        gate = _dot(kmh, qh) + _dot(kml, qh) + _dot(kmh, ql)
        gates.append(jnp.where(blk < cur.astype(f32), gate, NEG_INF))
    sels = [[] for _ in heads]
    for _ in range(MOBA_TOPK):
        for h in heads:
            best = jnp.max(gates[h], axis=0, keepdims=True)
            idx = jnp.min(jnp.where(gates[h] == best, blk, float(nblk)), axis=0, keepdims=True)
            idx = jnp.where(best > NEG_INF, idx, -1.0)
            sels[h].append(idx)
            gates[h] = jnp.where(blk == idx, NEG_INF, gates[h])

    qb = [(q_t[h] * (HEAD_DIM ** -0.5 * LOG2E)).astype(bf16) for h in heads]

    def block_scores(h, j):
        jc = jnp.minimum(j, nblk - 1)
        dist = jnp.clip(cur - jc, 0, N_BIAS_TILES - 1)
        kj = k_ref[pl.ds(pl.multiple_of(jc * MOBA_BLOCK, MOBA_BLOCK), MOBA_BLOCK), hcols[h]]
        return _dot(kj, qb[h]) + bias_ref[h, dist]

    def block_values(h, j):
        jc = jnp.minimum(j, nblk - 1)
        return vt_ref[0, hcols[h], pl.ds(pl.multiple_of(jc * MOBA_BLOCK, MOBA_BLOCK), MOBA_BLOCK)]

    key = lax.broadcasted_iota(jnp.int32, (MOBA_BLOCK, MOBA_BLOCK), 0)
    qry = lax.broadcasted_iota(jnp.int32, (MOBA_BLOCK, MOBA_BLOCK), 1)
    init = []
    for h in heads:
        s = jnp.where(qry >= key, block_scores(h, cur), NEG_INF)
        m0 = jnp.max(s, axis=0, keepdims=True)
        p = jnp.exp2(s - m0)
        l0 = jnp.sum(p, axis=0, keepdims=True)
        acc0 = _dot(block_values(h, cur), p.astype(bf16))
        init.append((m0, l0, acc0))

    units = [(h, u) for h in heads for u in range(MOBA_UNROLL)]

    def stash_scores(h, u, j):
        s = block_scores(h, j)
        s_scr[h, u] = s
        smax_scr[h, u] = jnp.max(s, axis=0, keepdims=True)

    def pending_pv(h, first_block):
        pv = None
        for u in range(MOBA_UNROLL):
            t = _dot(block_values(h, jnp.maximum(first_block + u, 0)), p_scr[h, u])
            pv = t if pv is None else pv + t
        return pv

    for h, u in units:
        stash_scores(h, u, jnp.int32(u))
        p_scr[h, u] = jnp.zeros((MOBA_BLOCK, MOBA_BLOCK), bf16)

    def softmax_group(h, base, m_prev, l_prev):
        chosen = []
        for u in range(MOBA_UNROLL):
            j = base + u
            jf = j.astype(f32)
            chosen.append(((sels[h][0] == jf) | (sels[h][1] == jf) | (sels[h][2] == jf)) & (j < cur))
        m_new = m_prev
        for u in range(MOBA_UNROLL):
            m_new = jnp.maximum(m_new, jnp.where(chosen[u], smax_scr[h, u], NEG_INF))
        alpha = jnp.exp2(m_prev - m_new)
        l_new = alpha * l_prev
        ps = []
        for u in range(MOBA_UNROLL):
            p = jnp.exp2(s_scr[h, u] - jnp.where(chosen[u], m_new, float("inf")))
            l_new = l_new + jnp.sum(p, axis=0, keepdims=True)
            ps.append(p.astype(bf16))
        return m_new, l_new, alpha, ps

    def step(it, state):
        base = it * MOBA_UNROLL
        pv = [pending_pv(h, base - MOBA_UNROLL) for h in heads]
        out = []
        for h in heads:
            m_prev, l_prev, acc = state[h]
            m_new, l_new, alpha, ps = softmax_group(h, base, m_prev, l_prev)
            for u in range(MOBA_UNROLL):
                p_scr[h, u] = ps[u]
            out.append((m_new, l_new, alpha * (acc + pv[h])))
        for h, u in units:
            stash_scores(h, u, base + MOBA_UNROLL + u)
        return tuple(out)

    def finish(h, l_fin, acc):
        z = z_ref[:, hcols[h]].astype(f32)
        o_ref[:, hcols[h]] = ((acc / l_fin).T * (z * jax.nn.sigmoid(z))).astype(bf16)

    n_steps = (cur + MOBA_UNROLL - 1) // MOBA_UNROLL
    state = lax.fori_loop(0, jnp.maximum(n_steps - 1, 0), step, tuple(init))

    @pl.when(n_steps >= 1)
    def _():
        base = (n_steps - 1) * MOBA_UNROLL
        for h in heads:
            m_prev, l_prev, acc = state[h]
            pv = pending_pv(h, base - MOBA_UNROLL)
            _, l_fin, alpha, ps = softmax_group(h, base, m_prev, l_prev)
            acc = alpha * (acc + pv)
            for u in range(MOBA_UNROLL):
                acc = acc + _dot(block_values(h, base + u), ps[u])
            finish(h, l_fin, acc)

    @pl.when(n_steps == 0)
    def _():
        for h in heads:
            finish(h, state[h][1], state[h][2])


def _moba_attention(q, z, k, v_t, k_mean, bias, bsz, seq, name):
    m = bsz * seq
    nq = seq // MOBA_BLOCK
    hp = MOBA_HEADS_PER_STEP
    w = hp * HEAD_DIM
    return pl.pallas_call(
        _moba_kernel,
        grid=(bsz, HEADS // hp, nq),
        in_specs=[
            pl.BlockSpec((MOBA_BLOCK, w), lambda b, g, i: (b * nq + i, g)),
            pl.BlockSpec((MOBA_BLOCK, w), lambda b, g, i: (b * nq + i, g)),
            pl.BlockSpec((seq, w), lambda b, g, i: (b, g)),
            pl.BlockSpec((1, w, seq), lambda b, g, i: (b, g, 0)),
            pl.BlockSpec((1, nq, w), lambda b, g, i: (b, 0, g)),
            pl.BlockSpec((hp, N_BIAS_TILES, MOBA_BLOCK, MOBA_BLOCK), lambda b, g, i: (g, 0, 0, 0)),
        ],
        out_specs=pl.BlockSpec((MOBA_BLOCK, w), lambda b, g, i: (b * nq + i, g)),
        out_shape=jax.ShapeDtypeStruct((m, KEY_W), bf16),
        scratch_shapes=[pltpu.VMEM((hp, MOBA_UNROLL, MOBA_BLOCK, MOBA_BLOCK), f32),
                        pltpu.VMEM((hp, MOBA_UNROLL, 1, MOBA_BLOCK), f32),
                        pltpu.VMEM((hp, MOBA_UNROLL, MOBA_BLOCK, MOBA_BLOCK), bf16)],
        compiler_params=_cparams(("parallel", "parallel", "arbitrary")),
        name=name,
    )(q, z, k, v_t, k_mean, bias)


def kernel(x, a_norm_g, a_w_in, a_conv_w, a_log, a_dt_bias, a_out_norm_g, a_w_out, kv_norm_g, w_kv, b_norm_g, b_w_in, b_w_out, rel_bias, final_norm_g):
    bsz, seq, d = x.shape
    m = bsz * seq
    assert d == D_MODEL and seq % MOBA_BLOCK == 0 and seq % GDN_TB == 0 and m % MM_TM == 0 and m % GATES_TM == 0
    xf = x.reshape(m, d).astype(f32)

    qkvz_w = CONV_CH + KEY_W
    for i in range(a_w_in.shape[0]):
        w_ab = jnp.pad(a_w_in[i, :, qkvz_w:], ((0, 0), (0, LANES - 2 * HEADS)))
        proj, ab = _norm_matmul(xf, a_norm_g[i], a_w_in, i, qkvz_w, GDN_TN, bf16,
                                f"gdn{i}_in_proj", w_side=w_ab)
        cols = _gdn_gates(ab[0], a_log[i], a_dt_bias[i])
        o = _gdn_core(proj, a_conv_w[i].astype(f32), cols, a_out_norm_g[i].astype(f32),
                      bsz, seq, f"gdn{i}_core")
        xf = _out_proj(o, a_w_out, i, xf, final_norm_g, False, f"gdn{i}_out_proj")

    k, v_t, k_mean = _kv_proj(xf, kv_norm_g, w_kv, bsz, seq)
    k_mean = k_mean.reshape(bsz, seq // MOBA_BLOCK, KEY_W)
    bias = _bias_tiles(rel_bias)

    n_b = b_w_in.shape[0]
    for j in range(n_b):
        q, z = _moba_in_proj(xf, b_norm_g[j], b_w_in, j, f"moba{j}_in_proj")
        o = _moba_attention(q, z, k, v_t, k_mean, bias, bsz, seq, f"moba{j}_attn")
        xf = _out_proj(o, b_w_out, j, xf, final_norm_g, j == n_b - 1, f"moba{j}_out_proj")
    return xf.reshape(bsz, seq, d).astype(x.dtype)
```

```python
import functools
import math

import jax
import jax.numpy as jnp
from jax import lax
from jax.experimental import pallas as pl
from jax.experimental.pallas import tpu as pltpu

f32 = jnp.float32
bf16 = jnp.bfloat16

D_MODEL = 1024
HEADS = 8
HEAD_DIM = 128
KEY_W = HEADS * HEAD_DIM
CONV_CH = 3 * KEY_W
CONV_W = 4
GDN_IN_W = CONV_CH + KEY_W + 2 * HEADS
CHUNK = 64
MOBA_BLOCK = 256
MOBA_TOPK = 3
N_BUCKETS = 32
MAX_DIST = 2048
EPS = 1e-6
NEG_INF = float("-inf")
LOG2E = math.log2(math.e)

LANES = 128
SUBLANES = 8
VMEM_LIMIT_BYTES = 56 * 1024 * 1024

MM_TM = 1024
GDN_TN = 2048
MOBA_IN_TM = 512
KV_TM = 512
GATES_TM = 2048
GDN_TB = 256
GDN_GROUP = 4
CONV_BLOCK = 256
MOBA_UNROLL = 4
MOBA_HEADS_PER_STEP = 2
N_BIAS_TILES = MAX_DIST // MOBA_BLOCK + 2


def _cparams(sem):
    return pltpu.CompilerParams(dimension_semantics=sem, vmem_limit_bytes=VMEM_LIMIT_BYTES)


def _dot(a, b):
    return jnp.dot(a, b, preferred_element_type=f32)


def _dot_nt(a, b):
    return lax.dot_general(a, b, (((1,), (1,)), ((), ())), preferred_element_type=f32)


def _dot_tn(a, b):
    return lax.dot_general(a, b, (((0,), (0,)), ((), ())), preferred_element_type=f32)


def _split(x):
    hi = x.astype(bf16)
    lo = (x - hi.astype(f32)).astype(bf16)
    return hi, lo


def _rmsnorm_bf16(x_ref, g_ref, rows=slice(None)):
    x = x_ref[rows, :]
    y = x * lax.rsqrt(jnp.mean(x * x, axis=-1, keepdims=True) + EPS)
    return (y * g_ref[...]).astype(bf16)


def _cast_weights_once(step, w_ref, wb_ref):
    @pl.when(step == 0)
    def _():
        wb_ref[...] = w_ref[...].astype(bf16)


def _gdn_in_kernel(x_ref, g_ref, w_ref, ws_ref, o_ref, os_ref, wb_ref):
    _cast_weights_once(pl.program_id(1), w_ref, wb_ref)
    o_ref[...] = _dot(_rmsnorm_bf16(x_ref, g_ref), wb_ref[...]).astype(o_ref.dtype)
    side_rows = os_ref.shape[0]
    rows = pl.ds(pl.multiple_of(pl.program_id(0) * side_rows, side_rows), side_rows)
    os_ref[...] = _dot(_rmsnorm_bf16(x_ref, g_ref, rows), ws_ref[...].astype(bf16))


def _gdn_in_proj(x, g, w_stack, layer, n, w_side, name):
    m, k = x.shape
    ns = w_side.shape[1]
    n_col = n // GDN_TN
    side_rows = MM_TM // n_col
    return pl.pallas_call(
        _gdn_in_kernel,
        grid=(n_col, m // MM_TM),
        in_specs=[
            pl.BlockSpec((MM_TM, k), lambda j, i: (i, 0)),
            pl.BlockSpec((1, k), lambda j, i: (0, 0)),
            pl.BlockSpec((None, k, GDN_TN), lambda j, i: (layer, 0, j)),
            pl.BlockSpec((k, ns), lambda j, i: (0, 0)),
        ],
        out_specs=[
            pl.BlockSpec((MM_TM, GDN_TN), lambda j, i: (i, j)),
            pl.BlockSpec((side_rows, ns), lambda j, i: (i * n_col + j, 0)),
        ],
        out_shape=[jax.ShapeDtypeStruct((m, n), bf16), jax.ShapeDtypeStruct((m, ns), f32)],
        scratch_shapes=[pltpu.VMEM((k, GDN_TN), bf16)],
        compiler_params=_cparams(("arbitrary", "arbitrary")),
        name=name,
    )(x, g.reshape(1, k).astype(f32), w_stack, w_side)


def _moba_in_kernel(x_ref, g_ref, w_ref, q_ref, z_ref, wb_ref):
    _cast_weights_once(pl.program_id(0), w_ref, wb_ref)
    xn = _rmsnorm_bf16(x_ref, g_ref)
    q_ref[...] = _dot(xn, wb_ref[:, :KEY_W])
    z_ref[...] = _dot(xn, wb_ref[:, KEY_W:]).astype(bf16)


def _moba_in_proj(x, g, w_stack, layer, name):
    m, k = x.shape
    return pl.pallas_call(
        _moba_in_kernel,
        grid=(m // MOBA_IN_TM,),
        in_specs=[
            pl.BlockSpec((MOBA_IN_TM, k), lambda i: (i, 0)),
            pl.BlockSpec((1, k), lambda i: (0, 0)),
            pl.BlockSpec((None, k, 2 * KEY_W), lambda i: (layer, 0, 0)),
        ],
        out_specs=[
            pl.BlockSpec((MOBA_IN_TM, KEY_W), lambda i: (i, 0)),
            pl.BlockSpec((MOBA_IN_TM, KEY_W), lambda i: (i, 0)),
        ],
        out_shape=[jax.ShapeDtypeStruct((m, KEY_W), f32), jax.ShapeDtypeStruct((m, KEY_W), bf16)],
        scratch_shapes=[pltpu.VMEM((k, 2 * KEY_W), bf16)],
        compiler_params=_cparams(("arbitrary",)),
        name=name,
    )(x, g.reshape(1, k).astype(f32), w_stack)


def _kv_kernel(x_ref, g_ref, w_ref, k_ref, vt_ref, km_ref, wb_ref):
    _cast_weights_once(pl.program_id(0), w_ref, wb_ref)
    acc = _dot(_rmsnorm_bf16(x_ref, g_ref), wb_ref[...])
    k = acc[:, :KEY_W]
    k_ref[...] = k.astype(bf16)
    for blk in range(KV_TM // MOBA_BLOCK):
        km_ref[blk] = jnp.mean(k[blk * MOBA_BLOCK:(blk + 1) * MOBA_BLOCK], axis=0, keepdims=True)
    vt_ref[0] = acc[:, KEY_W:].T.astype(bf16)


def _kv_proj(x, g, w, bsz, seq):
    m, k = x.shape
    nt = seq // KV_TM
    blocks_per_tile = KV_TM // MOBA_BLOCK
    return pl.pallas_call(
        _kv_kernel,
        grid=(m // KV_TM,),
        in_specs=[
            pl.BlockSpec((KV_TM, k), lambda i: (i, 0)),
            pl.BlockSpec((1, k), lambda i: (0, 0)),
            pl.BlockSpec((k, 2 * KEY_W), lambda i: (0, 0)),
        ],
        out_specs=[
            pl.BlockSpec((KV_TM, KEY_W), lambda i: (i, 0)),
            pl.BlockSpec((1, KEY_W, KV_TM), lambda i: (i // nt, 0, i % nt)),
            pl.BlockSpec((blocks_per_tile, 1, KEY_W), lambda i: (i, 0, 0)),
        ],
        out_shape=[
            jax.ShapeDtypeStruct((m, KEY_W), bf16),
            jax.ShapeDtypeStruct((bsz, KEY_W, seq), bf16),
            jax.ShapeDtypeStruct((m // MOBA_BLOCK, 1, KEY_W), f32),
        ],
        scratch_shapes=[pltpu.VMEM((k, 2 * KEY_W), bf16)],
        compiler_params=_cparams(("arbitrary",)),
        name="kv_proj",
    )(x, g.reshape(1, k).astype(f32), w)


def _out_kernel(a_ref, w_ref, r_ref, g_ref, o_ref, wb_ref, *, final_norm):
    _cast_weights_once(pl.program_id(0), w_ref, wb_ref)
    y = r_ref[...] + _dot(a_ref[...], wb_ref[...])
    if final_norm:
        y = y * lax.rsqrt(jnp.mean(y * y, axis=-1, keepdims=True) + EPS) * g_ref[...]
    o_ref[...] = y


def _out_proj(a, w_stack, layer, res, g, final_norm, name):
    m, k = a.shape
    n = w_stack.shape[2]
    return pl.pallas_call(
        functools.partial(_out_kernel, final_norm=final_norm),
        grid=(m // MM_TM,),
        in_specs=[
            pl.BlockSpec((MM_TM, k), lambda i: (i, 0)),
            pl.BlockSpec((None, k, n), lambda i: (layer, 0, 0)),
            pl.BlockSpec((MM_TM, n), lambda i: (i, 0)),
            pl.BlockSpec((1, n), lambda i: (0, 0)),
        ],
        out_specs=pl.BlockSpec((MM_TM, n), lambda i: (i, 0)),
        out_shape=jax.ShapeDtypeStruct((m, n), f32),
        scratch_shapes=[pltpu.VMEM((k, n), bf16)],
        compiler_params=_cparams(("arbitrary",)),
        name=name,
    )(a, w_stack, res, g.reshape(1, n).astype(f32))


def _gates_kernel(ab_ref, alog_ref, dt_ref, o_ref):
    ab = ab_ref[...]
    rows = ab.shape[0]
    x = ab + dt_ref[...]
    softplus = jnp.maximum(x, 0.0) + jnp.log1p(jnp.exp(-jnp.abs(x)))
    g = -jnp.exp(alog_ref[...]) * softplus
    pos = lax.broadcasted_iota(jnp.int32, ab.shape, 0) % CHUNK
    fwd = g
    bwd = g
    s = 1
    while s < CHUNK:
        fwd = fwd + jnp.where(pos >= s, pltpu.roll(fwd, s, 0), 0.0)
        bwd = bwd + jnp.where(pos < CHUNK - s, pltpu.roll(bwd, rows - s, 0), 0.0)
        s *= 2
    lane = lax.broadcasted_iota(jnp.int32, ab.shape, 1)
    out = jnp.where(lane < HEADS, fwd, jax.nn.sigmoid(ab))
    out = jnp.where(lane < 2 * HEADS, out, pltpu.roll(jnp.exp(fwd), 2 * HEADS, 1))
    out = jnp.where(lane < 3 * HEADS, out, pltpu.roll(jnp.exp(bwd - g), 3 * HEADS, 1))
    o_ref[...] = jnp.where(lane < 4 * HEADS, out, 0.0)


def _gdn_gates(ab, a_log, dt_bias):
    m = ab.shape[0]
    lane_pad = (0, LANES - HEADS)
    alog = jnp.pad(a_log.astype(f32), lane_pad).reshape(1, LANES)
    dtb = jnp.pad(dt_bias.astype(f32), lane_pad).reshape(1, LANES)
    tile = pl.BlockSpec((GATES_TM, LANES), lambda i: (i, 0))
    vec = pl.BlockSpec((1, LANES), lambda i: (0, 0))
    return pl.pallas_call(
        _gates_kernel,
        grid=(m // GATES_TM,),
        in_specs=[tile, vec, vec],
        out_specs=tile,
        out_shape=jax.ShapeDtypeStruct((m, LANES), f32),
        compiler_params=_cparams(("parallel",)),
        name="gdn_gates",
    )(ab, alog, dtb)


def _neumann_inverse(mats):
    n = mats[0].shape[0]
    row = lax.broadcasted_iota(jnp.int32, (n, n), 0)
    col = lax.broadcasted_iota(jnp.int32, (n, n), 1)
    eye = jnp.where(row == col, 1.0, 0.0).astype(f32)

    ts = [eye - a for a in mats]
    xbs = [a.astype(bf16) for a in mats]
    p = 1
    while True:
        xbs = [_dot(xb, xb).astype(bf16) for xb in xbs]
        p *= 2
        ts = [t + _dot(t.astype(bf16), xb) for t, xb in zip(ts, xbs)]
        if 2 * p >= n:
            return ts


def _gdn_kernel(x_ref, z_ref, cw_ref, cols_ref, og_ref, o_ref,
                halo, qkvn, state):
    tb = x_ref.shape[0]
    t = pl.program_id(1)

    @pl.when(t == 0)
    def _():
        halo[...] = jnp.zeros_like(halo)
        state[...] = jnp.zeros_like(state)

    trow = lax.broadcasted_iota(jnp.int32, (tb, tb), 0)
    tcol = lax.broadcasted_iota(jnp.int32, (tb, tb), 1)
    shifts = [jnp.where(trow - tcol == s, 1.0, 0.0).astype(bf16) for s in range(1, CONV_W)]
    hrow = lax.broadcasted_iota(jnp.int32, (SUBLANES, CONV_BLOCK), 0)
    for cb in range(CONV_CH // CONV_BLOCK):
        cs = slice(cb * CONV_BLOCK, (cb + 1) * CONV_BLOCK)
        xb = x_ref[:, cs]
        acc = xb.astype(f32) * cw_ref[CONV_W - 1:CONV_W, cs]
        patch = jnp.zeros((SUBLANES, CONV_BLOCK), f32)
        for s in range(1, CONV_W):
            w_s = cw_ref[CONV_W - 1 - s:CONV_W - s, cs]
            acc = acc + _dot(shifts[s - 1], xb) * w_s
            patch = patch + jnp.where(hrow < s, pltpu.roll(halo[:, cs], s, 0), 0.0) * w_s
        acc = jnp.concatenate([acc[:SUBLANES] + patch, acc[SUBLANES:]], axis=0)
        y = acc * jax.nn.sigmoid(acc)
        for half in range(CONV_BLOCK // LANES):
            lane_block = cb * (CONV_BLOCK // LANES) + half
            yh = y[:, half * LANES:(half + 1) * LANES]
            if lane_block < 2 * HEADS:
                yh = yh * lax.rsqrt(jnp.sum(yh * yh, axis=-1, keepdims=True) + EPS)
                if lane_block < HEADS:
                    yh = yh * (HEAD_DIM ** -0.5)
            qkvn[:, lane_block * LANES:(lane_block + 1) * LANES] = yh
    halo[...] = x_ref[tb - 2 * SUBLANES:tb, :].astype(f32)[SUBLANES:]

    row = lax.broadcasted_iota(jnp.int32, (CHUNK, CHUNK), 0)
    col = lax.broadcasted_iota(jnp.int32, (CHUNK, CHUNK), 1)
    tril = row >= col
    strict = row > col
    og = og_ref[...]

    hs = range(HEADS)
    qcols = [slice(h * HEAD_DIM, (h + 1) * HEAD_DIM) for h in hs]

    def group_step(gi, carry):
        items = [(ci, h) for ci in range(GDN_GROUP) for h in hs]
        rows, g_rows, ctile = [], [], []
        for ci in range(GDN_GROUP):
            c = gi * GDN_GROUP + ci
            rows.append(pl.ds(pl.multiple_of(c * CHUNK, CHUNK), CHUNK))
            ctile.append(cols_ref[0, c])
            g_rows.append(ctile[ci].T)

        def col(ci, h, which):
            return ctile[ci][:, which * HEADS + h:which * HEADS + h + 1]

        q = {it: qkvn[rows[it[0]], qcols[it[1]]] for it in items}
        k = {(ci, h): qkvn[rows[ci], KEY_W + h * HEAD_DIM:KEY_W + (h + 1) * HEAD_DIM] for ci, h in items}
        v = {(ci, h): qkvn[rows[ci], 2 * KEY_W + h * HEAD_DIM:2 * KEY_W + (h + 1) * HEAD_DIM]
             for ci, h in items}
        decay = {(ci, h): jnp.exp(jnp.where(tril, col(ci, h, 0) - g_rows[ci][h:h + 1, :], NEG_INF))
                 for ci, h in items}
        k_beta = {it: k[it] * col(*it, 1) for it in items}
        kq = {it: _dot_nt(jnp.concatenate([k_beta[it], q[it]], axis=0).astype(bf16), k[it].astype(bf16))
              for it in items}
        t_inv = dict(zip(items, _neumann_inverse(
            [jnp.where(strict, kq[it][:CHUNK] * decay[it], 0.0) for it in items])))
        uw = {it: _dot(t_inv[it].astype(bf16),
                       jnp.concatenate([v[it] * col(*it, 1), k_beta[it] * col(*it, 2)], axis=1).astype(bf16))
              for it in items}
        w_qd = {it: jnp.concatenate([uw[it][:, HEAD_DIM:], q[it] * col(*it, 2)], axis=0).astype(bf16)
                for it in items}
        qk = {it: (kq[it][CHUNK:] * decay[it]).astype(bf16) for it in items}
        k_dec = {it: (k[it] * col(*it, 3)).astype(bf16) for it in items}

        for ci in range(GDN_GROUP):
            s_prev = [state[h] for h in hs]
            ws_qs = [_dot(w_qd[ci, h], s_prev[h].astype(bf16)) for h in hs]
            v_new = [(uw[ci, h][:, :HEAD_DIM] - ws_qs[h][:CHUNK]).astype(bf16) for h in hs]
            o = [ws_qs[h][CHUNK:] + _dot(qk[ci, h], v_new[h]) for h in hs]
            for h in hs:
                g_last = col(ci, h, 2)[CHUNK - 1:CHUNK, :]
                state[h] = s_prev[h] * g_last + _dot_tn(k_dec[ci, h], v_new[h])
            for h in hs:
                on = o[h] * lax.rsqrt(jnp.mean(o[h] * o[h], axis=-1, keepdims=True) + EPS) * og
                z = z_ref[rows[ci], qcols[h]].astype(f32)
                o_ref[rows[ci], qcols[h]] = (on * (z * jax.nn.sigmoid(z))).astype(bf16)
        return carry

    lax.fori_loop(0, tb // (CHUNK * GDN_GROUP), group_step, 0)


def _gdn_core(proj, conv_w, cols, out_norm_g, bsz, seq, name):
    m = bsz * seq
    nt = seq // GDN_TB
    cpt = GDN_TB // CHUNK
    return pl.pallas_call(
        _gdn_kernel,
        grid=(bsz, nt),
        in_specs=[
            pl.BlockSpec((GDN_TB, CONV_CH), lambda b, t: (b * nt + t, 0)),
            pl.BlockSpec((GDN_TB, KEY_W), lambda b, t: (b * nt + t, CONV_CH // KEY_W)),
            pl.BlockSpec((CONV_W, CONV_CH), lambda b, t: (0, 0)),
            pl.BlockSpec((1, cpt, CHUNK, LANES), lambda b, t: (b, t, 0, 0)),
            pl.BlockSpec((1, HEAD_DIM), lambda b, t: (0, 0)),
        ],
        out_specs=pl.BlockSpec((GDN_TB, KEY_W), lambda b, t: (b * nt + t, 0)),
        out_shape=jax.ShapeDtypeStruct((m, KEY_W), bf16),
        scratch_shapes=[
            pltpu.VMEM((SUBLANES, CONV_CH), f32),
            pltpu.VMEM((GDN_TB, CONV_CH), f32),
            pltpu.VMEM((HEADS, HEAD_DIM, HEAD_DIM), f32),
        ],
        compiler_params=_cparams(("parallel", "arbitrary")),
        name=name,
    )(proj, proj, conv_w, cols.reshape(bsz, seq // CHUNK, CHUNK, LANES), out_norm_g.reshape(1, HEAD_DIM))


def _t5_bucket_host(n):
    max_exact = N_BUCKETS // 2
    if n < max_exact:
        return n
    large = max_exact + int(math.log(n / max_exact) / math.log(MAX_DIST / max_exact) * (N_BUCKETS - max_exact))
    return min(large, N_BUCKETS - 1)


def _bias_kernel(rb_ref, o_ref):
    h = pl.program_id(0)
    key = lax.broadcasted_iota(jnp.int32, (MOBA_BLOCK, MOBA_BLOCK), 0)
    qry = lax.broadcasted_iota(jnp.int32, (MOBA_BLOCK, MOBA_BLOCK), 1)
    max_exact = N_BUCKETS // 2
    for d in range(N_BIAS_TILES):
        n = jnp.maximum(d * MOBA_BLOCK + qry - key, 0)
        nf = jnp.maximum(n, 1).astype(f32)
        large = max_exact + (jnp.log(nf / max_exact) / math.log(MAX_DIST / max_exact)
                             * (N_BUCKETS - max_exact)).astype(jnp.int32)
        large = jnp.minimum(large, N_BUCKETS - 1)
        bucket = jnp.where(n < max_exact, n, large)
        b_lo = max(_t5_bucket_host(max(d * MOBA_BLOCK - (MOBA_BLOCK - 1), 0)) - 1, 0)
        b_hi = min(_t5_bucket_host(d * MOBA_BLOCK + MOBA_BLOCK - 1) + 1, N_BUCKETS - 1)
        out = jnp.zeros((MOBA_BLOCK, MOBA_BLOCK), f32)
        for b in range(b_lo, b_hi + 1):
            out = jnp.where(bucket == b, rb_ref[h, b], out)
        o_ref[0, d] = out * LOG2E


def _bias_tiles(rel_bias):
    return pl.pallas_call(
        _bias_kernel,
        grid=(HEADS,),
        in_specs=[pl.BlockSpec(memory_space=pltpu.SMEM)],
        out_specs=pl.BlockSpec((1, N_BIAS_TILES, MOBA_BLOCK, MOBA_BLOCK), lambda h: (h, 0, 0, 0)),
        out_shape=jax.ShapeDtypeStruct((HEADS, N_BIAS_TILES, MOBA_BLOCK, MOBA_BLOCK), f32),
        compiler_params=_cparams(("parallel",)),
        name="t5_bias_tiles",
    )(rel_bias.T.astype(f32))


def _moba_kernel(q_ref, z_ref, k_ref, vt_ref, km_ref, bias_ref, o_ref, s_scr, smax_scr, p_scr):
    cur = pl.program_id(2)
    nblk = km_ref.shape[1]
    heads = range(MOBA_HEADS_PER_STEP)
    hcols = [slice(h * HEAD_DIM, (h + 1) * HEAD_DIM) for h in heads]
    q_t = [q_ref[:, cs].T for cs in hcols]

    blk = lax.broadcasted_iota(jnp.int32, (nblk, MOBA_BLOCK), 0).astype(f32)
    gates = []
    for h in heads:
        qh, ql = _split(q_t[h])
        kmh, kml = _split(km_ref[0, :, hcols[h]])
        gate = _dot(kmh, qh) + _dot(kml, qh) + _dot(kmh, ql)
        gates.append(jnp.where(blk < cur.astype(f32), gate, NEG_INF))
    sels = [[] for _ in heads]
    for _ in range(MOBA_TOPK):
        for h in heads:
            best = jnp.max(gates[h], axis=0, keepdims=True)
            idx = jnp.min(jnp.where(gates[h] == best, blk, float(nblk)), axis=0, keepdims=True)
            idx = jnp.where(best > NEG_INF, idx, -1.0)
            sels[h].append(idx)
            gates[h] = jnp.where(blk == idx, NEG_INF, gates[h])

    qb = [(q_t[h] * (HEAD_DIM ** -0.5 * LOG2E)).astype(bf16) for h in heads]

    def block_scores(h, j):
        jc = jnp.minimum(j, nblk - 1)
        dist = jnp.clip(cur - jc, 0, N_BIAS_TILES - 1)
        kj = k_ref[pl.ds(pl.multiple_of(jc * MOBA_BLOCK, MOBA_BLOCK), MOBA_BLOCK), hcols[h]]
        return _dot(kj, qb[h]) + bias_ref[h, dist]

    def block_values(h, j):
        jc = jnp.minimum(j, nblk - 1)
        return vt_ref[0, hcols[h], pl.ds(pl.multiple_of(jc * MOBA_BLOCK, MOBA_BLOCK), MOBA_BLOCK)]

    key = lax.broadcasted_iota(jnp.int32, (MOBA_BLOCK, MOBA_BLOCK), 0)
    qry = lax.broadcasted_iota(jnp.int32, (MOBA_BLOCK, MOBA_BLOCK), 1)
    init = []
    for h in heads:
        s = jnp.where(qry >= key, block_scores(h, cur), NEG_INF)
        m0 = jnp.max(s, axis=0, keepdims=True)
        p = jnp.exp2(s - m0)
        l0 = jnp.sum(p, axis=0, keepdims=True)
        p_scr[h, 0] = p.astype(bf16)
        init.append((m0, l0, jnp.zeros((HEAD_DIM, MOBA_BLOCK), f32)))

    units = [(h, u) for h in heads for u in range(MOBA_UNROLL)]

    def stash_scores(h, u, j):
        s = block_scores(h, j)
        s_scr[h, u] = s
        smax_scr[h, u] = jnp.max(s, axis=0, keepdims=True)

    def pending_pv(h, it):
        first_block = (it - 1) * MOBA_UNROLL
        pv = _dot(block_values(h, jnp.where(it == 0, cur, first_block)), p_scr[h, 0])
        for u in range(1, MOBA_UNROLL):
            pv = pv + _dot(block_values(h, jnp.maximum(first_block + u, 0)), p_scr[h, u])
        return pv

    for h, u in units:
        stash_scores(h, u, jnp.int32(u))
        if u > 0:
            p_scr[h, u] = jnp.zeros((MOBA_BLOCK, MOBA_BLOCK), bf16)

    def softmax_group(h, base, m_prev, l_prev):
        chosen = []
        for u in range(MOBA_UNROLL):
            j = base + u
            jf = j.astype(f32)
            chosen.append(((sels[h][0] == jf) | (sels[h][1] == jf) | (sels[h][2] == jf)) & (j < cur))
        m_new = m_prev
        for u in range(MOBA_UNROLL):
            m_new = jnp.maximum(m_new, jnp.where(chosen[u], smax_scr[h, u], NEG_INF))
        alpha = jnp.exp2(m_prev - m_new)
        l_new = alpha * l_prev
        ps = []
        for u in range(MOBA_UNROLL):
            p = jnp.exp2(s_scr[h, u] - jnp.where(chosen[u], m_new, float("inf")))
            l_new = l_new + jnp.sum(p, axis=0, keepdims=True)
            ps.append(p.astype(bf16))
        return m_new, l_new, alpha, ps

    def step(it, state):
        base = it * MOBA_UNROLL
        pv = [pending_pv(h, it) for h in heads]
        out = []
        for h in heads:
            m_prev, l_prev, acc = state[h]
            m_new, l_new, alpha, ps = softmax_group(h, base, m_prev, l_prev)
            for u in range(MOBA_UNROLL):
                p_scr[h, u] = ps[u]
            out.append((m_new, l_new, alpha * (acc + pv[h])))
        for h, u in units:
            stash_scores(h, u, base + MOBA_UNROLL + u)
        return tuple(out)

    def finish(h, l_fin, acc):
        z = z_ref[:, hcols[h]].astype(f32)
        o_ref[:, hcols[h]] = ((acc / l_fin).T * (z * jax.nn.sigmoid(z))).astype(bf16)

    n_steps = (cur + MOBA_UNROLL - 1) // MOBA_UNROLL
    state = lax.fori_loop(0, jnp.maximum(n_steps - 1, 0), step, tuple(init))

    @pl.when(n_steps >= 1)
    def _():
        base = (n_steps - 1) * MOBA_UNROLL
        for h in heads:
            m_prev, l_prev, acc = state[h]
            pv = pending_pv(h, n_steps - 1)
            _, l_fin, alpha, ps = softmax_group(h, base, m_prev, l_prev)
            acc = alpha * (acc + pv)
            for u in range(MOBA_UNROLL):
                acc = acc + _dot(block_values(h, base + u), ps[u])
            finish(h, l_fin, acc)

    @pl.when(n_steps == 0)
    def _():
        for h in heads:
            finish(h, state[h][1], pending_pv(h, jnp.int32(0)))


def _moba_attention(q, z, k, v_t, k_mean, bias, bsz, seq, name):
    m = bsz * seq
    nq = seq // MOBA_BLOCK
    hp = MOBA_HEADS_PER_STEP
    w = hp * HEAD_DIM
    return pl.pallas_call(
        _moba_kernel,
        grid=(bsz, HEADS // hp, nq),
        in_specs=[
            pl.BlockSpec((MOBA_BLOCK, w), lambda b, g, i: (b * nq + i, g)),
            pl.BlockSpec((MOBA_BLOCK, w), lambda b, g, i: (b * nq + i, g)),
            pl.BlockSpec((seq, w), lambda b, g, i: (b, g)),
            pl.BlockSpec((1, w, seq), lambda b, g, i: (b, g, 0)),
            pl.BlockSpec((1, nq, w), lambda b, g, i: (b, 0, g)),
            pl.BlockSpec((hp, N_BIAS_TILES, MOBA_BLOCK, MOBA_BLOCK), lambda b, g, i: (g, 0, 0, 0)),
        ],
        out_specs=pl.BlockSpec((MOBA_BLOCK, w), lambda b, g, i: (b * nq + i, g)),
        out_shape=jax.ShapeDtypeStruct((m, KEY_W), bf16),
        scratch_shapes=[pltpu.VMEM((hp, MOBA_UNROLL, MOBA_BLOCK, MOBA_BLOCK), f32),
                        pltpu.VMEM((hp, MOBA_UNROLL, 1, MOBA_BLOCK), f32),
                        pltpu.VMEM((hp, MOBA_UNROLL, MOBA_BLOCK, MOBA_BLOCK), bf16)],
        compiler_params=_cparams(("parallel", "parallel", "arbitrary")),
        name=name,
    )(q, z, k, v_t, k_mean, bias)


def kernel(x, a_norm_g, a_w_in, a_conv_w, a_log, a_dt_bias, a_out_norm_g, a_w_out, kv_norm_g, w_kv, b_norm_g, b_w_in, b_w_out, rel_bias, final_norm_g):
    bsz, seq, d = x.shape
    m = bsz * seq
    assert d == D_MODEL and seq % MOBA_BLOCK == 0 and seq % GDN_TB == 0 and m % MM_TM == 0 and m % GATES_TM == 0
    xf = x.reshape(m, d).astype(f32)

    qkvz_w = CONV_CH + KEY_W
    for i in range(a_w_in.shape[0]):
        w_ab = jnp.pad(a_w_in[i, :, qkvz_w:], ((0, 0), (0, LANES - 2 * HEADS)))
        proj, ab = _gdn_in_proj(xf, a_norm_g[i], a_w_in, i, qkvz_w, w_ab, f"gdn{i}_in_proj")
        cols = _gdn_gates(ab, a_log[i], a_dt_bias[i])
        o = _gdn_core(proj, a_conv_w[i].astype(f32), cols, a_out_norm_g[i].astype(f32),
                      bsz, seq, f"gdn{i}_core")
        xf = _out_proj(o, a_w_out, i, xf, final_norm_g, False, f"gdn{i}_out_proj")

    k, v_t, k_mean = _kv_proj(xf, kv_norm_g, w_kv, bsz, seq)
    k_mean = k_mean.reshape(bsz, seq // MOBA_BLOCK, KEY_W)
    bias = _bias_tiles(rel_bias)

    n_b = b_w_in.shape[0]
    for j in range(n_b):
        q, z = _moba_in_proj(xf, b_norm_g[j], b_w_in, j, f"moba{j}_in_proj")
        o = _moba_attention(q, z, k, v_t, k_mean, bias, bsz, seq, f"moba{j}_attn")
        xf = _out_proj(o, b_w_out, j, xf, final_norm_g, j == n_b - 1, f"moba{j}_out_proj")
    return xf.reshape(bsz, seq, d).astype(x.dtype)
```

```python
import functools
import math

import jax
import jax.numpy as jnp
from jax import lax
from jax.experimental import pallas as pl
from jax.experimental.pallas import tpu as pltpu

f32 = jnp.float32
bf16 = jnp.bfloat16

D_MODEL = 1024
HEADS = 8
HEAD_DIM = 128
KEY_W = HEADS * HEAD_DIM
CONV_CH = 3 * KEY_W
CONV_W = 4
GDN_IN_W = CONV_CH + KEY_W + 2 * HEADS
CHUNK = 64
MOBA_BLOCK = 256
MOBA_TOPK = 3
N_BUCKETS = 32
MAX_DIST = 2048
EPS = 1e-6
NEG_INF = float("-inf")
LOG2E = math.log2(math.e)

LANES = 128
SUBLANES = 8
VMEM_LIMIT_BYTES = 56 * 1024 * 1024

MM_TM = 1024
GDN_TN = 2048
MOBA_IN_TM = 512
KV_TM = 512
GATES_TM = 2048
GDN_TB = 256
GDN_GROUP = 4
CONV_BLOCK = 256
MOBA_UNROLL = 4
MOBA_HEADS_PER_STEP = 2
MOBA_QBLOCKS = 2
N_BIAS_TILES = MAX_DIST // MOBA_BLOCK + 2


def _cparams(sem):
    return pltpu.CompilerParams(dimension_semantics=sem, vmem_limit_bytes=VMEM_LIMIT_BYTES)


def _dot(a, b):
    return jnp.dot(a, b, preferred_element_type=f32)


def _dot_nt(a, b):
    return lax.dot_general(a, b, (((1,), (1,)), ((), ())), preferred_element_type=f32)


def _dot_tn(a, b):
    return lax.dot_general(a, b, (((0,), (0,)), ((), ())), preferred_element_type=f32)


def _split(x):
    hi = x.astype(bf16)
    lo = (x - hi.astype(f32)).astype(bf16)
    return hi, lo


def _rmsnorm_bf16(x_ref, g_ref, rows=slice(None)):
    x = x_ref[rows, :]
    y = x * lax.rsqrt(jnp.mean(x * x, axis=-1, keepdims=True) + EPS)
    return (y * g_ref[...]).astype(bf16)


def _cast_weights_once(step, w_ref, wb_ref):
    @pl.when(step == 0)
    def _():
        wb_ref[...] = w_ref[...].astype(bf16)


def _gdn_in_kernel(x_ref, g_ref, w_ref, ws_ref, o_ref, os_ref, wb_ref):
    _cast_weights_once(pl.program_id(1), w_ref, wb_ref)
    o_ref[...] = _dot(_rmsnorm_bf16(x_ref, g_ref), wb_ref[...]).astype(o_ref.dtype)
    side_rows = os_ref.shape[0]
    rows = pl.ds(pl.multiple_of(pl.program_id(0) * side_rows, side_rows), side_rows)
    os_ref[...] = _dot(_rmsnorm_bf16(x_ref, g_ref, rows), ws_ref[...].astype(bf16))


def _gdn_in_proj(x, g, w_stack, layer, n, w_side, name):
    m, k = x.shape
    ns = w_side.shape[1]
    n_col = n // GDN_TN
    side_rows = MM_TM // n_col
    return pl.pallas_call(
        _gdn_in_kernel,
        grid=(n_col, m // MM_TM),
        in_specs=[
            pl.BlockSpec((MM_TM, k), lambda j, i: (i, 0)),
            pl.BlockSpec((1, k), lambda j, i: (0, 0)),
            pl.BlockSpec((None, k, GDN_TN), lambda j, i: (layer, 0, j)),
            pl.BlockSpec((k, ns), lambda j, i: (0, 0)),
        ],
        out_specs=[
            pl.BlockSpec((MM_TM, GDN_TN), lambda j, i: (i, j)),
            pl.BlockSpec((side_rows, ns), lambda j, i: (i * n_col + j, 0)),
        ],
        out_shape=[jax.ShapeDtypeStruct((m, n), bf16), jax.ShapeDtypeStruct((m, ns), f32)],
        scratch_shapes=[pltpu.VMEM((k, GDN_TN), bf16)],
        compiler_params=_cparams(("arbitrary", "arbitrary")),
        name=name,
    )(x, g.reshape(1, k).astype(f32), w_stack, w_side)


def _moba_in_kernel(x_ref, g_ref, w_ref, q_ref, z_ref, wb_ref):
    _cast_weights_once(pl.program_id(0), w_ref, wb_ref)
    xn = _rmsnorm_bf16(x_ref, g_ref)
    q_ref[...] = _dot(xn, wb_ref[:, :KEY_W])
    z_ref[...] = _dot(xn, wb_ref[:, KEY_W:]).astype(bf16)


def _moba_in_proj(x, g, w_stack, layer, name):
    m, k = x.shape
    return pl.pallas_call(
        _moba_in_kernel,
        grid=(m // MOBA_IN_TM,),
        in_specs=[
            pl.BlockSpec((MOBA_IN_TM, k), lambda i: (i, 0)),
            pl.BlockSpec((1, k), lambda i: (0, 0)),
            pl.BlockSpec((None, k, 2 * KEY_W), lambda i: (layer, 0, 0)),
        ],
        out_specs=[
            pl.BlockSpec((MOBA_IN_TM, KEY_W), lambda i: (i, 0)),
            pl.BlockSpec((MOBA_IN_TM, KEY_W), lambda i: (i, 0)),
        ],
        out_shape=[jax.ShapeDtypeStruct((m, KEY_W), f32), jax.ShapeDtypeStruct((m, KEY_W), bf16)],
        scratch_shapes=[pltpu.VMEM((k, 2 * KEY_W), bf16)],
        compiler_params=_cparams(("arbitrary",)),
        name=name,
    )(x, g.reshape(1, k).astype(f32), w_stack)


def _kv_kernel(x_ref, g_ref, w_ref, k_ref, vt_ref, km_ref, wb_ref):
    _cast_weights_once(pl.program_id(0), w_ref, wb_ref)
    acc = _dot(_rmsnorm_bf16(x_ref, g_ref), wb_ref[...])
    k = acc[:, :KEY_W]
    k_ref[...] = k.astype(bf16)
    for blk in range(KV_TM // MOBA_BLOCK):
        km_ref[blk] = jnp.mean(k[blk * MOBA_BLOCK:(blk + 1) * MOBA_BLOCK], axis=0, keepdims=True)
    vt_ref[0] = acc[:, KEY_W:].T.astype(bf16)


def _kv_proj(x, g, w, bsz, seq):
    m, k = x.shape
    nt = seq // KV_TM
    blocks_per_tile = KV_TM // MOBA_BLOCK
    return pl.pallas_call(
        _kv_kernel,
        grid=(m // KV_TM,),
        in_specs=[
            pl.BlockSpec((KV_TM, k), lambda i: (i, 0)),
            pl.BlockSpec((1, k), lambda i: (0, 0)),
            pl.BlockSpec((k, 2 * KEY_W), lambda i: (0, 0)),
        ],
        out_specs=[
            pl.BlockSpec((KV_TM, KEY_W), lambda i: (i, 0)),
            pl.BlockSpec((1, KEY_W, KV_TM), lambda i: (i // nt, 0, i % nt)),
            pl.BlockSpec((blocks_per_tile, 1, KEY_W), lambda i: (i, 0, 0)),
        ],
        out_shape=[
            jax.ShapeDtypeStruct((m, KEY_W), bf16),
            jax.ShapeDtypeStruct((bsz, KEY_W, seq), bf16),
            jax.ShapeDtypeStruct((m // MOBA_BLOCK, 1, KEY_W), f32),
        ],
        scratch_shapes=[pltpu.VMEM((k, 2 * KEY_W), bf16)],
        compiler_params=_cparams(("arbitrary",)),
        name="kv_proj",
    )(x, g.reshape(1, k).astype(f32), w)


def _out_kernel(a_ref, w_ref, r_ref, g_ref, o_ref, wb_ref, *, final_norm):
    _cast_weights_once(pl.program_id(0), w_ref, wb_ref)
    y = r_ref[...] + _dot(a_ref[...], wb_ref[...])
    if final_norm:
        y = y * lax.rsqrt(jnp.mean(y * y, axis=-1, keepdims=True) + EPS) * g_ref[...]
    o_ref[...] = y


def _out_proj(a, w_stack, layer, res, g, final_norm, name):
    m, k = a.shape
    n = w_stack.shape[2]
    return pl.pallas_call(
        functools.partial(_out_kernel, final_norm=final_norm),
        grid=(m // MM_TM,),
        in_specs=[
            pl.BlockSpec((MM_TM, k), lambda i: (i, 0)),
            pl.BlockSpec((None, k, n), lambda i: (layer, 0, 0)),
            pl.BlockSpec((MM_TM, n), lambda i: (i, 0)),
            pl.BlockSpec((1, n), lambda i: (0, 0)),
        ],
        out_specs=pl.BlockSpec((MM_TM, n), lambda i: (i, 0)),
        out_shape=jax.ShapeDtypeStruct((m, n), f32),
        scratch_shapes=[pltpu.VMEM((k, n), bf16)],
        compiler_params=_cparams(("arbitrary",)),
        name=name,
    )(a, w_stack, res, g.reshape(1, n).astype(f32))


def _gates_kernel(ab_ref, alog_ref, dt_ref, o_ref):
    ab = ab_ref[...]
    rows = ab.shape[0]
    x = ab + dt_ref[...]
    softplus = jnp.maximum(x, 0.0) + jnp.log1p(jnp.exp(-jnp.abs(x)))
    g = -jnp.exp(alog_ref[...]) * softplus
    pos = lax.broadcasted_iota(jnp.int32, ab.shape, 0) % CHUNK
    fwd = g
    bwd = g
    s = 1
    while s < CHUNK:
        fwd = fwd + jnp.where(pos >= s, pltpu.roll(fwd, s, 0), 0.0)
        bwd = bwd + jnp.where(pos < CHUNK - s, pltpu.roll(bwd, rows - s, 0), 0.0)
        s *= 2
    lane = lax.broadcasted_iota(jnp.int32, ab.shape, 1)
    out = jnp.where(lane < HEADS, fwd, jax.nn.sigmoid(ab))
    out = jnp.where(lane < 2 * HEADS, out, pltpu.roll(jnp.exp(fwd), 2 * HEADS, 1))
    out = jnp.where(lane < 3 * HEADS, out, pltpu.roll(jnp.exp(bwd - g), 3 * HEADS, 1))
    o_ref[...] = jnp.where(lane < 4 * HEADS, out, 0.0)


def _gdn_gates(ab, a_log, dt_bias):
    m = ab.shape[0]
    lane_pad = (0, LANES - HEADS)
    alog = jnp.pad(a_log.astype(f32), lane_pad).reshape(1, LANES)
    dtb = jnp.pad(dt_bias.astype(f32), lane_pad).reshape(1, LANES)
    tile = pl.BlockSpec((GATES_TM, LANES), lambda i: (i, 0))
    vec = pl.BlockSpec((1, LANES), lambda i: (0, 0))
    return pl.pallas_call(
        _gates_kernel,
        grid=(m // GATES_TM,),
        in_specs=[tile, vec, vec],
        out_specs=tile,
        out_shape=jax.ShapeDtypeStruct((m, LANES), f32),
        compiler_params=_cparams(("parallel",)),
        name="gdn_gates",
    )(ab, alog, dtb)


def _neumann_inverse(mats):
    n = mats[0].shape[0]
    row = lax.broadcasted_iota(jnp.int32, (n, n), 0)
    col = lax.broadcasted_iota(jnp.int32, (n, n), 1)
    eye = jnp.where(row == col, 1.0, 0.0).astype(f32)

    ts = [eye - a for a in mats]
    xbs = [a.astype(bf16) for a in mats]
    p = 1
    while True:
        xbs = [_dot(xb, xb).astype(bf16) for xb in xbs]
        p *= 2
        ts = [t + _dot(t.astype(bf16), xb) for t, xb in zip(ts, xbs)]
        if 2 * p >= n:
            return ts


def _gdn_kernel(x_ref, z_ref, cw_ref, cols_ref, og_ref, o_ref,
                halo, qkvn, state):
    tb = x_ref.shape[0]
    t = pl.program_id(1)

    @pl.when(t == 0)
    def _():
        halo[...] = jnp.zeros_like(halo)
        state[...] = jnp.zeros_like(state)

    trow = lax.broadcasted_iota(jnp.int32, (tb, tb), 0)
    tcol = lax.broadcasted_iota(jnp.int32, (tb, tb), 1)
    shifts = [jnp.where(trow - tcol == s, 1.0, 0.0).astype(bf16) for s in range(1, CONV_W)]
    hrow = lax.broadcasted_iota(jnp.int32, (SUBLANES, CONV_BLOCK), 0)
    for cb in range(CONV_CH // CONV_BLOCK):
        cs = slice(cb * CONV_BLOCK, (cb + 1) * CONV_BLOCK)
        xb = x_ref[:, cs]
        acc = xb.astype(f32) * cw_ref[CONV_W - 1:CONV_W, cs]
        patch = jnp.zeros((SUBLANES, CONV_BLOCK), f32)
        for s in range(1, CONV_W):
            w_s = cw_ref[CONV_W - 1 - s:CONV_W - s, cs]
            acc = acc + _dot(shifts[s - 1], xb) * w_s
            patch = patch + jnp.where(hrow < s, pltpu.roll(halo[:, cs], s, 0), 0.0) * w_s
        acc = jnp.concatenate([acc[:SUBLANES] + patch, acc[SUBLANES:]], axis=0)
        y = acc * jax.nn.sigmoid(acc)
        for half in range(CONV_BLOCK // LANES):
            lane_block = cb * (CONV_BLOCK // LANES) + half
            yh = y[:, half * LANES:(half + 1) * LANES]
            if lane_block < 2 * HEADS:
                yh = yh * lax.rsqrt(jnp.sum(yh * yh, axis=-1, keepdims=True) + EPS)
                if lane_block < HEADS:
                    yh = yh * (HEAD_DIM ** -0.5)
            qkvn[:, lane_block * LANES:(lane_block + 1) * LANES] = yh
    halo[...] = x_ref[tb - 2 * SUBLANES:tb, :].astype(f32)[SUBLANES:]

    row = lax.broadcasted_iota(jnp.int32, (CHUNK, CHUNK), 0)
    col = lax.broadcasted_iota(jnp.int32, (CHUNK, CHUNK), 1)
    tril = row >= col
    strict = row > col
    og = og_ref[...]

    hs = range(HEADS)
    qcols = [slice(h * HEAD_DIM, (h + 1) * HEAD_DIM) for h in hs]

    def group_step(gi, carry):
        items = [(ci, h) for ci in range(GDN_GROUP) for h in hs]
        rows, g_rows, ctile = [], [], []
        for ci in range(GDN_GROUP):
            c = gi * GDN_GROUP + ci
            rows.append(pl.ds(pl.multiple_of(c * CHUNK, CHUNK), CHUNK))
            ctile.append(cols_ref[0, c])
            g_rows.append(ctile[ci].T)

        def col(ci, h, which):
            return ctile[ci][:, which * HEADS + h:which * HEADS + h + 1]

        q = {it: qkvn[rows[it[0]], qcols[it[1]]] for it in items}
        k = {(ci, h): qkvn[rows[ci], KEY_W + h * HEAD_DIM:KEY_W + (h + 1) * HEAD_DIM] for ci, h in items}
        v = {(ci, h): qkvn[rows[ci], 2 * KEY_W + h * HEAD_DIM:2 * KEY_W + (h + 1) * HEAD_DIM]
             for ci, h in items}
        decay = {(ci, h): jnp.exp(jnp.where(tril, col(ci, h, 0) - g_rows[ci][h:h + 1, :], NEG_INF))
                 for ci, h in items}
        k_beta = {it: k[it] * col(*it, 1) for it in items}
        kq = {it: _dot_nt(jnp.concatenate([k_beta[it], q[it]], axis=0).astype(bf16), k[it].astype(bf16))
              for it in items}
        t_inv = dict(zip(items, _neumann_inverse(
            [jnp.where(strict, kq[it][:CHUNK] * decay[it], 0.0) for it in items])))
        uw = {it: _dot(t_inv[it].astype(bf16),
                       jnp.concatenate([v[it] * col(*it, 1), k_beta[it] * col(*it, 2)], axis=1).astype(bf16))
              for it in items}
        w_qd = {it: jnp.concatenate([uw[it][:, HEAD_DIM:], q[it] * col(*it, 2)], axis=0).astype(bf16)
                for it in items}
        qk = {it: (kq[it][CHUNK:] * decay[it]).astype(bf16) for it in items}
        k_dec = {it: (k[it] * col(*it, 3)).astype(bf16) for it in items}

        for ci in range(GDN_GROUP):
            s_prev = [state[h] for h in hs]
            ws_qs = [_dot(w_qd[ci, h], s_prev[h].astype(bf16)) for h in hs]
            v_new = [(uw[ci, h][:, :HEAD_DIM] - ws_qs[h][:CHUNK]).astype(bf16) for h in hs]
            o = [ws_qs[h][CHUNK:] + _dot(qk[ci, h], v_new[h]) for h in hs]
            for h in hs:
                g_last = col(ci, h, 2)[CHUNK - 1:CHUNK, :]
                state[h] = s_prev[h] * g_last + _dot_tn(k_dec[ci, h], v_new[h])
            for h in hs:
                on = o[h] * lax.rsqrt(jnp.mean(o[h] * o[h], axis=-1, keepdims=True) + EPS) * og
                z = z_ref[rows[ci], qcols[h]].astype(f32)
                o_ref[rows[ci], qcols[h]] = (on * (z * jax.nn.sigmoid(z))).astype(bf16)
        return carry

    lax.fori_loop(0, tb // (CHUNK * GDN_GROUP), group_step, 0)


def _gdn_core(proj, conv_w, cols, out_norm_g, bsz, seq, name):
    m = bsz * seq
    nt = seq // GDN_TB
    cpt = GDN_TB // CHUNK
    return pl.pallas_call(
        _gdn_kernel,
        grid=(bsz, nt),
        in_specs=[
            pl.BlockSpec((GDN_TB, CONV_CH), lambda b, t: (b * nt + t, 0)),
            pl.BlockSpec((GDN_TB, KEY_W), lambda b, t: (b * nt + t, CONV_CH // KEY_W)),
            pl.BlockSpec((CONV_W, CONV_CH), lambda b, t: (0, 0)),
            pl.BlockSpec((1, cpt, CHUNK, LANES), lambda b, t: (b, t, 0, 0)),
            pl.BlockSpec((1, HEAD_DIM), lambda b, t: (0, 0)),
        ],
        out_specs=pl.BlockSpec((GDN_TB, KEY_W), lambda b, t: (b * nt + t, 0)),
        out_shape=jax.ShapeDtypeStruct((m, KEY_W), bf16),
        scratch_shapes=[
            pltpu.VMEM((SUBLANES, CONV_CH), f32),
            pltpu.VMEM((GDN_TB, CONV_CH), f32),
            pltpu.VMEM((HEADS, HEAD_DIM, HEAD_DIM), f32),
        ],
        compiler_params=_cparams(("parallel", "arbitrary")),
        name=name,
    )(proj, proj, conv_w, cols.reshape(bsz, seq // CHUNK, CHUNK, LANES), out_norm_g.reshape(1, HEAD_DIM))


def _t5_bucket_host(n):
    max_exact = N_BUCKETS // 2
    if n < max_exact:
        return n
    large = max_exact + int(math.log(n / max_exact) / math.log(MAX_DIST / max_exact) * (N_BUCKETS - max_exact))
    return min(large, N_BUCKETS - 1)


def _bias_kernel(rb_ref, o_ref):
    h = pl.program_id(0)
    key = lax.broadcasted_iota(jnp.int32, (MOBA_BLOCK, MOBA_BLOCK), 0)
    qry = lax.broadcasted_iota(jnp.int32, (MOBA_BLOCK, MOBA_BLOCK), 1)
    max_exact = N_BUCKETS // 2
    for d in range(N_BIAS_TILES):
        n = jnp.maximum(d * MOBA_BLOCK + qry - key, 0)
        nf = jnp.maximum(n, 1).astype(f32)
        large = max_exact + (jnp.log(nf / max_exact) / math.log(MAX_DIST / max_exact)
                             * (N_BUCKETS - max_exact)).astype(jnp.int32)
        large = jnp.minimum(large, N_BUCKETS - 1)
        bucket = jnp.where(n < max_exact, n, large)
        b_lo = max(_t5_bucket_host(max(d * MOBA_BLOCK - (MOBA_BLOCK - 1), 0)) - 1, 0)
        b_hi = min(_t5_bucket_host(d * MOBA_BLOCK + MOBA_BLOCK - 1) + 1, N_BUCKETS - 1)
        out = jnp.zeros((MOBA_BLOCK, MOBA_BLOCK), f32)
        for b in range(b_lo, b_hi + 1):
            out = jnp.where(bucket == b, rb_ref[h, b], out)
        o_ref[0, d] = out * LOG2E


def _bias_tiles(rel_bias):
    return pl.pallas_call(
        _bias_kernel,
        grid=(HEADS,),
        in_specs=[pl.BlockSpec(memory_space=pltpu.SMEM)],
        out_specs=pl.BlockSpec((1, N_BIAS_TILES, MOBA_BLOCK, MOBA_BLOCK), lambda h: (h, 0, 0, 0)),
        out_shape=jax.ShapeDtypeStruct((HEADS, N_BIAS_TILES, MOBA_BLOCK, MOBA_BLOCK), f32),
        compiler_params=_cparams(("parallel",)),
        name="t5_bias_tiles",
    )(rel_bias.T.astype(f32))


def _moba_kernel(q_ref, z_ref, k_ref, vt_ref, km_ref, bias_ref, o_ref, s_scr, smax_scr, p_scr):
    pair = pl.program_id(2)
    nblk = km_ref.shape[1]
    curs = [MOBA_QBLOCKS * pair + t for t in range(MOBA_QBLOCKS)]
    qrows = [slice(t * MOBA_BLOCK, (t + 1) * MOBA_BLOCK) for t in range(MOBA_QBLOCKS)]
    units = [(t, h) for t in range(MOBA_QBLOCKS) for h in range(MOBA_HEADS_PER_STEP)]
    hcols = [slice(h * HEAD_DIM, (h + 1) * HEAD_DIM) for h in range(MOBA_HEADS_PER_STEP)]
    q_t = {(t, h): q_ref[qrows[t], hcols[h]].T for t, h in units}

    blk = lax.broadcasted_iota(jnp.int32, (nblk, MOBA_BLOCK), 0).astype(f32)
    gates = {}
    for t, h in units:
        qh, ql = _split(q_t[t, h])
        kmh, kml = _split(km_ref[0, :, hcols[h]])
        gate = _dot(kmh, qh) + _dot(kml, qh) + _dot(kmh, ql)
        gates[t, h] = jnp.where(blk < curs[t].astype(f32), gate, NEG_INF)
    sels = {u: [] for u in units}
    for _ in range(MOBA_TOPK):
        for u in units:
            best = jnp.max(gates[u], axis=0, keepdims=True)
            idx = jnp.min(jnp.where(gates[u] == best, blk, float(nblk)), axis=0, keepdims=True)
            idx = jnp.where(best > NEG_INF, idx, -1.0)
            sels[u].append(idx)
            gates[u] = jnp.where(blk == idx, NEG_INF, gates[u])

    qs = {u: (q_t[u] * (HEAD_DIM ** -0.5 * LOG2E)).astype(bf16) for u in units}

    def block_scores(u, j):
        t, h = u
        jc = jnp.minimum(j, nblk - 1)
        dist = jnp.clip(curs[t] - jc, 0, N_BIAS_TILES - 1)
        kj = k_ref[pl.ds(pl.multiple_of(jc * MOBA_BLOCK, MOBA_BLOCK), MOBA_BLOCK), hcols[h]]
        return _dot(kj, qs[u]) + bias_ref[h, dist]

    def block_values(u, j):
        jc = jnp.minimum(j, nblk - 1)
        return vt_ref[0, hcols[u[1]], pl.ds(pl.multiple_of(jc * MOBA_BLOCK, MOBA_BLOCK), MOBA_BLOCK)]

    key = lax.broadcasted_iota(jnp.int32, (MOBA_BLOCK, MOBA_BLOCK), 0)
    qry = lax.broadcasted_iota(jnp.int32, (MOBA_BLOCK, MOBA_BLOCK), 1)
    init = []
    for ui, u in enumerate(units):
        s = jnp.where(qry >= key, block_scores(u, curs[u[0]]), NEG_INF)
        m0 = jnp.max(s, axis=0, keepdims=True)
        p = jnp.exp2(s - m0)
        l0 = jnp.sum(p, axis=0, keepdims=True)
        p_scr[ui, 0] = p.astype(bf16)
        init.append((m0, l0, jnp.zeros((HEAD_DIM, MOBA_BLOCK), f32)))

    slots = [(ui, s) for ui in range(len(units)) for s in range(MOBA_UNROLL)]

    def stash_scores(ui, slot, j):
        s = block_scores(units[ui], j)
        s_scr[ui, slot] = s
        smax_scr[ui, slot] = jnp.max(s, axis=0, keepdims=True)

    def pending_pv(ui, it):
        u = units[ui]
        first_block = (it - 1) * MOBA_UNROLL
        pv = _dot(block_values(u, jnp.where(it == 0, curs[u[0]], first_block)), p_scr[ui, 0])
        for s in range(1, MOBA_UNROLL):
            pv = pv + _dot(block_values(u, jnp.maximum(first_block + s, 0)), p_scr[ui, s])
        return pv

    for ui, s in slots:
        stash_scores(ui, s, jnp.int32(s))
        if s > 0:
            p_scr[ui, s] = jnp.zeros((MOBA_BLOCK, MOBA_BLOCK), bf16)

    def softmax_group(ui, base, m_prev, l_prev):
        u = units[ui]
        chosen = []
        for s in range(MOBA_UNROLL):
            j = base + s
            jf = j.astype(f32)
            chosen.append(((sels[u][0] == jf) | (sels[u][1] == jf) | (sels[u][2] == jf))
                          & (j < curs[u[0]]))
        m_new = m_prev
        for s in range(MOBA_UNROLL):
            m_new = jnp.maximum(m_new, jnp.where(chosen[s], smax_scr[ui, s], NEG_INF))
        alpha = jnp.exp2(m_prev - m_new)
        l_new = alpha * l_prev
        ps = []
        for s in range(MOBA_UNROLL):
            p = jnp.exp2(s_scr[ui, s] - jnp.where(chosen[s], m_new, float("inf")))
            l_new = l_new + jnp.sum(p, axis=0, keepdims=True)
            ps.append(p.astype(bf16))
        return m_new, l_new, alpha, ps

    def step(it, state):
        base = it * MOBA_UNROLL
        pv = [pending_pv(ui, it) for ui in range(len(units))]
        out = []
        for ui in range(len(units)):
            m_prev, l_prev, acc = state[ui]
            m_new, l_new, alpha, ps = softmax_group(ui, base, m_prev, l_prev)
            for s in range(MOBA_UNROLL):
                p_scr[ui, s] = ps[s]
            out.append((m_new, l_new, alpha * (acc + pv[ui])))
        for ui, s in slots:
            stash_scores(ui, s, base + MOBA_UNROLL + s)
        return tuple(out)

    n_steps = (curs[-1] + MOBA_UNROLL - 1) // MOBA_UNROLL
    state = lax.fori_loop(0, n_steps - 1, step, tuple(init))
    base = (n_steps - 1) * MOBA_UNROLL
    pv = [pending_pv(ui, n_steps - 1) for ui in range(len(units))]
    last = [softmax_group(ui, base, state[ui][0], state[ui][1]) for ui in range(len(units))]
    for ui, (t, h) in enumerate(units):
        _, l_fin, alpha, ps = last[ui]
        acc = alpha * (state[ui][2] + pv[ui])
        for s in range(MOBA_UNROLL):
            acc = acc + _dot(block_values(units[ui], base + s), ps[s])
        z = z_ref[qrows[t], hcols[h]].astype(f32)
        o_ref[qrows[t], hcols[h]] = ((acc / l_fin).T * (z * jax.nn.sigmoid(z))).astype(bf16)


def _moba_attention(q, z, k, v_t, k_mean, bias, bsz, seq, name):
    m = bsz * seq
    nq = seq // MOBA_BLOCK
    npair = nq // MOBA_QBLOCKS
    hp = MOBA_HEADS_PER_STEP
    w = hp * HEAD_DIM
    n_units = MOBA_QBLOCKS * hp
    tile = pl.BlockSpec((MOBA_QBLOCKS * MOBA_BLOCK, w), lambda b, g, p: (b * npair + p, g))
    return pl.pallas_call(
        _moba_kernel,
        grid=(bsz, HEADS // hp, npair),
        in_specs=[
            tile, tile,
            pl.BlockSpec((seq, w), lambda b, g, p: (b, g)),
            pl.BlockSpec((1, w, seq), lambda b, g, p: (b, g, 0)),
            pl.BlockSpec((1, nq, w), lambda b, g, p: (b, 0, g)),
            pl.BlockSpec((hp, N_BIAS_TILES, MOBA_BLOCK, MOBA_BLOCK), lambda b, g, p: (g, 0, 0, 0)),
        ],
        out_specs=tile,
        out_shape=jax.ShapeDtypeStruct((m, KEY_W), bf16),
        scratch_shapes=[pltpu.VMEM((n_units, MOBA_UNROLL, MOBA_BLOCK, MOBA_BLOCK), f32),
                        pltpu.VMEM((n_units, MOBA_UNROLL, 1, MOBA_BLOCK), f32),
                        pltpu.VMEM((n_units, MOBA_UNROLL, MOBA_BLOCK, MOBA_BLOCK), bf16)],
        compiler_params=_cparams(("parallel", "parallel", "arbitrary")),
        name=name,
    )(q, z, k, v_t, k_mean, bias)


def kernel(x, a_norm_g, a_w_in, a_conv_w, a_log, a_dt_bias, a_out_norm_g, a_w_out, kv_norm_g, w_kv, b_norm_g, b_w_in, b_w_out, rel_bias, final_norm_g):
    bsz, seq, d = x.shape
    m = bsz * seq
    assert d == D_MODEL and seq % MOBA_BLOCK == 0 and seq % GDN_TB == 0 and m % MM_TM == 0 and m % GATES_TM == 0
    xf = x.reshape(m, d).astype(f32)

    qkvz_w = CONV_CH + KEY_W
    for i in range(a_w_in.shape[0]):
        w_ab = jnp.pad(a_w_in[i, :, qkvz_w:], ((0, 0), (0, LANES - 2 * HEADS)))
        proj, ab = _gdn_in_proj(xf, a_norm_g[i], a_w_in, i, qkvz_w, w_ab, f"gdn{i}_in_proj")
        cols = _gdn_gates(ab, a_log[i], a_dt_bias[i])
        o = _gdn_core(proj, a_conv_w[i].astype(f32), cols, a_out_norm_g[i].astype(f32),
                      bsz, seq, f"gdn{i}_core")
        xf = _out_proj(o, a_w_out, i, xf, final_norm_g, False, f"gdn{i}_out_proj")

    k, v_t, k_mean = _kv_proj(xf, kv_norm_g, w_kv, bsz, seq)
    k_mean = k_mean.reshape(bsz, seq // MOBA_BLOCK, KEY_W)
    bias = _bias_tiles(rel_bias)

    n_b = b_w_in.shape[0]
    for j in range(n_b):
        q, z = _moba_in_proj(xf, b_norm_g[j], b_w_in, j, f"moba{j}_in_proj")
        o = _moba_attention(q, z, k, v_t, k_mean, bias, bsz, seq, f"moba{j}_attn")
        xf = _out_proj(o, b_w_out, j, xf, final_norm_g, j == n_b - 1, f"moba{j}_out_proj")
    return xf.reshape(bsz, seq, d).astype(x.dtype)
```

```python
import functools
import math

import jax
import jax.numpy as jnp
from jax import lax
from jax.experimental import pallas as pl
from jax.experimental.pallas import tpu as pltpu

f32 = jnp.float32
bf16 = jnp.bfloat16

D_MODEL = 1024
HEADS = 8
HEAD_DIM = 128
KEY_W = HEADS * HEAD_DIM
CONV_CH = 3 * KEY_W
CONV_W = 4
GDN_IN_W = CONV_CH + KEY_W + 2 * HEADS
CHUNK = 64
MOBA_BLOCK = 256
MOBA_TOPK = 3
N_BUCKETS = 32
MAX_DIST = 2048
EPS = 1e-6
NEG_INF = float("-inf")
LOG2E = math.log2(math.e)

LANES = 128
SUBLANES = 8
VMEM_LIMIT_BYTES = 56 * 1024 * 1024

MM_TM = 1024
GDN_TN = 2048
MOBA_IN_TM = 512
KV_TM = 512
GATES_TM = 2048
GDN_TB = 256
GDN_GROUP = 4
CONV_BLOCK = 256
MOBA_UNROLL = 4
MOBA_HEADS_PER_STEP = 2
MOBA_QBLOCKS = 4
N_BIAS_TILES = MAX_DIST // MOBA_BLOCK + 2


def _cparams(sem):
    return pltpu.CompilerParams(dimension_semantics=sem, vmem_limit_bytes=VMEM_LIMIT_BYTES)


def _dot(a, b):
    return jnp.dot(a, b, preferred_element_type=f32)


def _dot_nt(a, b):
    return lax.dot_general(a, b, (((1,), (1,)), ((), ())), preferred_element_type=f32)


def _dot_tn(a, b):
    return lax.dot_general(a, b, (((0,), (0,)), ((), ())), preferred_element_type=f32)


def _split(x):
    hi = x.astype(bf16)
    lo = (x - hi.astype(f32)).astype(bf16)
    return hi, lo


def _rmsnorm_bf16(x_ref, g_ref, rows=slice(None)):
    x = x_ref[rows, :]
    y = x * lax.rsqrt(jnp.mean(x * x, axis=-1, keepdims=True) + EPS)
    return (y * g_ref[...]).astype(bf16)


def _cast_weights_once(step, w_ref, wb_ref):
    @pl.when(step == 0)
    def _():
        wb_ref[...] = w_ref[...].astype(bf16)


def _gdn_in_kernel(x_ref, g_ref, w_ref, ws_ref, o_ref, os_ref, wb_ref):
    _cast_weights_once(pl.program_id(1), w_ref, wb_ref)
    o_ref[...] = _dot(_rmsnorm_bf16(x_ref, g_ref), wb_ref[...]).astype(o_ref.dtype)
    side_rows = os_ref.shape[0]
    rows = pl.ds(pl.multiple_of(pl.program_id(0) * side_rows, side_rows), side_rows)
    os_ref[...] = _dot(_rmsnorm_bf16(x_ref, g_ref, rows), ws_ref[...].astype(bf16))


def _gdn_in_proj(x, g, w_stack, layer, n, w_side, name):
    m, k = x.shape
    ns = w_side.shape[1]
    n_col = n // GDN_TN
    side_rows = MM_TM // n_col
    return pl.pallas_call(
        _gdn_in_kernel,
        grid=(n_col, m // MM_TM),
        in_specs=[
            pl.BlockSpec((MM_TM, k), lambda j, i: (i, 0)),
            pl.BlockSpec((1, k), lambda j, i: (0, 0)),
            pl.BlockSpec((None, k, GDN_TN), lambda j, i: (layer, 0, j)),
            pl.BlockSpec((k, ns), lambda j, i: (0, 0)),
        ],
        out_specs=[
            pl.BlockSpec((MM_TM, GDN_TN), lambda j, i: (i, j)),
            pl.BlockSpec((side_rows, ns), lambda j, i: (i * n_col + j, 0)),
        ],
        out_shape=[jax.ShapeDtypeStruct((m, n), bf16), jax.ShapeDtypeStruct((m, ns), f32)],
        scratch_shapes=[pltpu.VMEM((k, GDN_TN), bf16)],
        compiler_params=_cparams(("arbitrary", "arbitrary")),
        name=name,
    )(x, g.reshape(1, k).astype(f32), w_stack, w_side)


def _moba_in_kernel(x_ref, g_ref, w_ref, q_ref, z_ref, wb_ref):
    _cast_weights_once(pl.program_id(0), w_ref, wb_ref)
    xn = _rmsnorm_bf16(x_ref, g_ref)
    q_ref[...] = _dot(xn, wb_ref[:, :KEY_W])
    z_ref[...] = _dot(xn, wb_ref[:, KEY_W:]).astype(bf16)


def _moba_in_proj(x, g, w_stack, layer, name):
    m, k = x.shape
    return pl.pallas_call(
        _moba_in_kernel,
        grid=(m // MOBA_IN_TM,),
        in_specs=[
            pl.BlockSpec((MOBA_IN_TM, k), lambda i: (i, 0)),
            pl.BlockSpec((1, k), lambda i: (0, 0)),
            pl.BlockSpec((None, k, 2 * KEY_W), lambda i: (layer, 0, 0)),
        ],
        out_specs=[
            pl.BlockSpec((MOBA_IN_TM, KEY_W), lambda i: (i, 0)),
            pl.BlockSpec((MOBA_IN_TM, KEY_W), lambda i: (i, 0)),
        ],
        out_shape=[jax.ShapeDtypeStruct((m, KEY_W), f32), jax.ShapeDtypeStruct((m, KEY_W), bf16)],
        scratch_shapes=[pltpu.VMEM((k, 2 * KEY_W), bf16)],
        compiler_params=_cparams(("arbitrary",)),
        name=name,
    )(x, g.reshape(1, k).astype(f32), w_stack)


def _kv_kernel(x_ref, g_ref, w_ref, k_ref, vt_ref, km_ref, wb_ref):
    _cast_weights_once(pl.program_id(0), w_ref, wb_ref)
    acc = _dot(_rmsnorm_bf16(x_ref, g_ref), wb_ref[...])
    k = acc[:, :KEY_W]
    k_ref[...] = k.astype(bf16)
    for blk in range(KV_TM // MOBA_BLOCK):
        km_ref[blk] = jnp.mean(k[blk * MOBA_BLOCK:(blk + 1) * MOBA_BLOCK], axis=0, keepdims=True)
    vt_ref[0] = acc[:, KEY_W:].T.astype(bf16)


def _kv_proj(x, g, w, bsz, seq):
    m, k = x.shape
    nt = seq // KV_TM
    blocks_per_tile = KV_TM // MOBA_BLOCK
    return pl.pallas_call(
        _kv_kernel,
        grid=(m // KV_TM,),
        in_specs=[
            pl.BlockSpec((KV_TM, k), lambda i: (i, 0)),
            pl.BlockSpec((1, k), lambda i: (0, 0)),
            pl.BlockSpec((k, 2 * KEY_W), lambda i: (0, 0)),
        ],
        out_specs=[
            pl.BlockSpec((KV_TM, KEY_W), lambda i: (i, 0)),
            pl.BlockSpec((1, KEY_W, KV_TM), lambda i: (i // nt, 0, i % nt)),
            pl.BlockSpec((blocks_per_tile, 1, KEY_W), lambda i: (i, 0, 0)),
        ],
        out_shape=[
            jax.ShapeDtypeStruct((m, KEY_W), bf16),
            jax.ShapeDtypeStruct((bsz, KEY_W, seq), bf16),
            jax.ShapeDtypeStruct((m // MOBA_BLOCK, 1, KEY_W), f32),
        ],
        scratch_shapes=[pltpu.VMEM((k, 2 * KEY_W), bf16)],
        compiler_params=_cparams(("arbitrary",)),
        name="kv_proj",
    )(x, g.reshape(1, k).astype(f32), w)


def _out_kernel(a_ref, w_ref, r_ref, g_ref, o_ref, wb_ref, *, final_norm):
    _cast_weights_once(pl.program_id(0), w_ref, wb_ref)
    y = r_ref[...] + _dot(a_ref[...], wb_ref[...])
    if final_norm:
        y = y * lax.rsqrt(jnp.mean(y * y, axis=-1, keepdims=True) + EPS) * g_ref[...]
    o_ref[...] = y


def _out_proj(a, w_stack, layer, res, g, final_norm, name):
    m, k = a.shape
    n = w_stack.shape[2]
    return pl.pallas_call(
        functools.partial(_out_kernel, final_norm=final_norm),
        grid=(m // MM_TM,),
        in_specs=[
            pl.BlockSpec((MM_TM, k), lambda i: (i, 0)),
            pl.BlockSpec((None, k, n), lambda i: (layer, 0, 0)),
            pl.BlockSpec((MM_TM, n), lambda i: (i, 0)),
            pl.BlockSpec((1, n), lambda i: (0, 0)),
        ],
        out_specs=pl.BlockSpec((MM_TM, n), lambda i: (i, 0)),
        out_shape=jax.ShapeDtypeStruct((m, n), f32),
        scratch_shapes=[pltpu.VMEM((k, n), bf16)],
        compiler_params=_cparams(("arbitrary",)),
        name=name,
    )(a, w_stack, res, g.reshape(1, n).astype(f32))


def _gates_kernel(ab_ref, alog_ref, dt_ref, o_ref):
    ab = ab_ref[...]
    rows = ab.shape[0]
    x = ab + dt_ref[...]
    softplus = jnp.maximum(x, 0.0) + jnp.log1p(jnp.exp(-jnp.abs(x)))
    g = -jnp.exp(alog_ref[...]) * softplus
    pos = lax.broadcasted_iota(jnp.int32, ab.shape, 0) % CHUNK
    fwd = g
    bwd = g
    s = 1
    while s < CHUNK:
        fwd = fwd + jnp.where(pos >= s, pltpu.roll(fwd, s, 0), 0.0)
        bwd = bwd + jnp.where(pos < CHUNK - s, pltpu.roll(bwd, rows - s, 0), 0.0)
        s *= 2
    lane = lax.broadcasted_iota(jnp.int32, ab.shape, 1)
    out = jnp.where(lane < HEADS, fwd, jax.nn.sigmoid(ab))
    out = jnp.where(lane < 2 * HEADS, out, pltpu.roll(jnp.exp(fwd), 2 * HEADS, 1))
    out = jnp.where(lane < 3 * HEADS, out, pltpu.roll(jnp.exp(bwd - g), 3 * HEADS, 1))
    o_ref[...] = jnp.where(lane < 4 * HEADS, out, 0.0)


def _gdn_gates(ab, a_log, dt_bias):
    m = ab.shape[0]
    lane_pad = (0, LANES - HEADS)
    alog = jnp.pad(a_log.astype(f32), lane_pad).reshape(1, LANES)
    dtb = jnp.pad(dt_bias.astype(f32), lane_pad).reshape(1, LANES)
    tile = pl.BlockSpec((GATES_TM, LANES), lambda i: (i, 0))
    vec = pl.BlockSpec((1, LANES), lambda i: (0, 0))
    return pl.pallas_call(
        _gates_kernel,
        grid=(m // GATES_TM,),
        in_specs=[tile, vec, vec],
        out_specs=tile,
        out_shape=jax.ShapeDtypeStruct((m, LANES), f32),
        compiler_params=_cparams(("parallel",)),
        name="gdn_gates",
    )(ab, alog, dtb)


def _neumann_inverse(mats):
    n = mats[0].shape[0]
    row = lax.broadcasted_iota(jnp.int32, (n, n), 0)
    col = lax.broadcasted_iota(jnp.int32, (n, n), 1)
    eye = jnp.where(row == col, 1.0, 0.0).astype(f32)

    ts = [eye - a for a in mats]
    xbs = [a.astype(bf16) for a in mats]
    p = 1
    while True:
        xbs = [_dot(xb, xb).astype(bf16) for xb in xbs]
        p *= 2
        ts = [t + _dot(t.astype(bf16), xb) for t, xb in zip(ts, xbs)]
        if 2 * p >= n:
            return ts


def _gdn_kernel(x_ref, z_ref, cw_ref, cols_ref, og_ref, o_ref,
                halo, qkvn, state):
    tb = x_ref.shape[0]
    t = pl.program_id(1)

    @pl.when(t == 0)
    def _():
        halo[...] = jnp.zeros_like(halo)
        state[...] = jnp.zeros_like(state)

    trow = lax.broadcasted_iota(jnp.int32, (tb, tb), 0)
    tcol = lax.broadcasted_iota(jnp.int32, (tb, tb), 1)
    shifts = [jnp.where(trow - tcol == s, 1.0, 0.0).astype(bf16) for s in range(1, CONV_W)]
    hrow = lax.broadcasted_iota(jnp.int32, (SUBLANES, CONV_BLOCK), 0)
    for cb in range(CONV_CH // CONV_BLOCK):
        cs = slice(cb * CONV_BLOCK, (cb + 1) * CONV_BLOCK)
        xb = x_ref[:, cs]
        acc = xb.astype(f32) * cw_ref[CONV_W - 1:CONV_W, cs]
        patch = jnp.zeros((SUBLANES, CONV_BLOCK), f32)
        for s in range(1, CONV_W):
            w_s = cw_ref[CONV_W - 1 - s:CONV_W - s, cs]
            acc = acc + _dot(shifts[s - 1], xb) * w_s
            patch = patch + jnp.where(hrow < s, pltpu.roll(halo[:, cs], s, 0), 0.0) * w_s
        acc = jnp.concatenate([acc[:SUBLANES] + patch, acc[SUBLANES:]], axis=0)
        y = acc * jax.nn.sigmoid(acc)
        for half in range(CONV_BLOCK // LANES):
            lane_block = cb * (CONV_BLOCK // LANES) + half
            yh = y[:, half * LANES:(half + 1) * LANES]
            if lane_block < 2 * HEADS:
                yh = yh * lax.rsqrt(jnp.sum(yh * yh, axis=-1, keepdims=True) + EPS)
                if lane_block < HEADS:
                    yh = yh * (HEAD_DIM ** -0.5)
            qkvn[:, lane_block * LANES:(lane_block + 1) * LANES] = yh
    halo[...] = x_ref[tb - 2 * SUBLANES:tb, :].astype(f32)[SUBLANES:]

    row = lax.broadcasted_iota(jnp.int32, (CHUNK, CHUNK), 0)
    col = lax.broadcasted_iota(jnp.int32, (CHUNK, CHUNK), 1)
    tril = row >= col
    strict = row > col
    og = og_ref[...]

    hs = range(HEADS)
    qcols = [slice(h * HEAD_DIM, (h + 1) * HEAD_DIM) for h in hs]

    def group_step(gi, carry):
        items = [(ci, h) for ci in range(GDN_GROUP) for h in hs]
        rows, g_rows, ctile = [], [], []
        for ci in range(GDN_GROUP):
            c = gi * GDN_GROUP + ci
            rows.append(pl.ds(pl.multiple_of(c * CHUNK, CHUNK), CHUNK))
            ctile.append(cols_ref[0, c])
            g_rows.append(ctile[ci].T)

        def col(ci, h, which):
            return ctile[ci][:, which * HEADS + h:which * HEADS + h + 1]

        q = {it: qkvn[rows[it[0]], qcols[it[1]]] for it in items}
        k = {(ci, h): qkvn[rows[ci], KEY_W + h * HEAD_DIM:KEY_W + (h + 1) * HEAD_DIM] for ci, h in items}
        v = {(ci, h): qkvn[rows[ci], 2 * KEY_W + h * HEAD_DIM:2 * KEY_W + (h + 1) * HEAD_DIM]
             for ci, h in items}
        decay = {(ci, h): jnp.exp(jnp.where(tril, col(ci, h, 0) - g_rows[ci][h:h + 1, :], NEG_INF))
                 for ci, h in items}
        k_beta = {it: k[it] * col(*it, 1) for it in items}
        kq = {it: _dot_nt(jnp.concatenate([k_beta[it], q[it]], axis=0).astype(bf16), k[it].astype(bf16))
              for it in items}
        t_inv = dict(zip(items, _neumann_inverse(
            [jnp.where(strict, kq[it][:CHUNK] * decay[it], 0.0) for it in items])))
        uw = {it: _dot(t_inv[it].astype(bf16),
                       jnp.concatenate([v[it] * col(*it, 1), k_beta[it] * col(*it, 2)], axis=1).astype(bf16))
              for it in items}
        w_qd = {it: jnp.concatenate([uw[it][:, HEAD_DIM:], q[it] * col(*it, 2)], axis=0).astype(bf16)
                for it in items}
        qk = {it: (kq[it][CHUNK:] * decay[it]).astype(bf16) for it in items}
        k_dec = {it: (k[it] * col(*it, 3)).astype(bf16) for it in items}

        for ci in range(GDN_GROUP):
            s_prev = [state[h] for h in hs]
            ws_qs = [_dot(w_qd[ci, h], s_prev[h].astype(bf16)) for h in hs]
            v_new = [(uw[ci, h][:, :HEAD_DIM] - ws_qs[h][:CHUNK]).astype(bf16) for h in hs]
            o = [ws_qs[h][CHUNK:] + _dot(qk[ci, h], v_new[h]) for h in hs]
            for h in hs:
                g_last = col(ci, h, 2)[CHUNK - 1:CHUNK, :]
                state[h] = s_prev[h] * g_last + _dot_tn(k_dec[ci, h], v_new[h])
            for h in hs:
                on = o[h] * lax.rsqrt(jnp.mean(o[h] * o[h], axis=-1, keepdims=True) + EPS) * og
                z = z_ref[rows[ci], qcols[h]].astype(f32)
                o_ref[rows[ci], qcols[h]] = (on * (z * jax.nn.sigmoid(z))).astype(bf16)
        return carry

    lax.fori_loop(0, tb // (CHUNK * GDN_GROUP), group_step, 0)


def _gdn_core(proj, conv_w, cols, out_norm_g, bsz, seq, name):
    m = bsz * seq
    nt = seq // GDN_TB
    cpt = GDN_TB // CHUNK
    return pl.pallas_call(
        _gdn_kernel,
        grid=(bsz, nt),
        in_specs=[
            pl.BlockSpec((GDN_TB, CONV_CH), lambda b, t: (b * nt + t, 0)),
            pl.BlockSpec((GDN_TB, KEY_W), lambda b, t: (b * nt + t, CONV_CH // KEY_W)),
            pl.BlockSpec((CONV_W, CONV_CH), lambda b, t: (0, 0)),
            pl.BlockSpec((1, cpt, CHUNK, LANES), lambda b, t: (b, t, 0, 0)),
            pl.BlockSpec((1, HEAD_DIM), lambda b, t: (0, 0)),
        ],
        out_specs=pl.BlockSpec((GDN_TB, KEY_W), lambda b, t: (b * nt + t, 0)),
        out_shape=jax.ShapeDtypeStruct((m, KEY_W), bf16),
        scratch_shapes=[
            pltpu.VMEM((SUBLANES, CONV_CH), f32),
            pltpu.VMEM((GDN_TB, CONV_CH), f32),
            pltpu.VMEM((HEADS, HEAD_DIM, HEAD_DIM), f32),
        ],
        compiler_params=_cparams(("parallel", "arbitrary")),
        name=name,
    )(proj, proj, conv_w, cols.reshape(bsz, seq // CHUNK, CHUNK, LANES), out_norm_g.reshape(1, HEAD_DIM))


def _t5_bucket_host(n):
    max_exact = N_BUCKETS // 2
    if n < max_exact:
        return n
    large = max_exact + int(math.log(n / max_exact) / math.log(MAX_DIST / max_exact) * (N_BUCKETS - max_exact))
    return min(large, N_BUCKETS - 1)


def _bias_kernel(rb_ref, o_ref):
    h = pl.program_id(0)
    key = lax.broadcasted_iota(jnp.int32, (MOBA_BLOCK, MOBA_BLOCK), 0)
    qry = lax.broadcasted_iota(jnp.int32, (MOBA_BLOCK, MOBA_BLOCK), 1)
    max_exact = N_BUCKETS // 2
    for d in range(N_BIAS_TILES):
        n = jnp.maximum(d * MOBA_BLOCK + qry - key, 0)
        nf = jnp.maximum(n, 1).astype(f32)
        large = max_exact + (jnp.log(nf / max_exact) / math.log(MAX_DIST / max_exact)
                             * (N_BUCKETS - max_exact)).astype(jnp.int32)
        large = jnp.minimum(large, N_BUCKETS - 1)
        bucket = jnp.where(n < max_exact, n, large)
        b_lo = max(_t5_bucket_host(max(d * MOBA_BLOCK - (MOBA_BLOCK - 1), 0)) - 1, 0)
        b_hi = min(_t5_bucket_host(d * MOBA_BLOCK + MOBA_BLOCK - 1) + 1, N_BUCKETS - 1)
        out = jnp.zeros((MOBA_BLOCK, MOBA_BLOCK), f32)
        for b in range(b_lo, b_hi + 1):
            out = jnp.where(bucket == b, rb_ref[h, b], out)
        o_ref[0, d] = out * LOG2E


def _bias_tiles(rel_bias):
    return pl.pallas_call(
        _bias_kernel,
        grid=(HEADS,),
        in_specs=[pl.BlockSpec(memory_space=pltpu.SMEM)],
        out_specs=pl.BlockSpec((1, N_BIAS_TILES, MOBA_BLOCK, MOBA_BLOCK), lambda h: (h, 0, 0, 0)),
        out_shape=jax.ShapeDtypeStruct((HEADS, N_BIAS_TILES, MOBA_BLOCK, MOBA_BLOCK), f32),
        compiler_params=_cparams(("parallel",)),
        name="t5_bias_tiles",
    )(rel_bias.T.astype(f32))


def _moba_kernel(q_ref, z_ref, k_ref, vt_ref, km_ref, bias_ref, o_ref, s_scr, smax_scr, p_scr):
    pair = pl.program_id(2)
    nblk = km_ref.shape[1]
    curs = [MOBA_QBLOCKS * pair + t for t in range(MOBA_QBLOCKS)]
    qrows = [slice(t * MOBA_BLOCK, (t + 1) * MOBA_BLOCK) for t in range(MOBA_QBLOCKS)]
    units = [(t, h) for t in range(MOBA_QBLOCKS) for h in range(MOBA_HEADS_PER_STEP)]
    hcols = [slice(h * HEAD_DIM, (h + 1) * HEAD_DIM) for h in range(MOBA_HEADS_PER_STEP)]
    q_t = {(t, h): q_ref[qrows[t], hcols[h]].T for t, h in units}

    blk = lax.broadcasted_iota(jnp.int32, (nblk, MOBA_BLOCK), 0).astype(f32)
    gates = {}
    for t, h in units:
        qh, ql = _split(q_t[t, h])
        kmh, kml = _split(km_ref[0, :, hcols[h]])
        gate = _dot(kmh, qh) + _dot(kml, qh) + _dot(kmh, ql)
        gates[t, h] = jnp.where(blk < curs[t].astype(f32), gate, NEG_INF)
    sels = {u: [] for u in units}
    for _ in range(MOBA_TOPK):
        for u in units:
            best = jnp.max(gates[u], axis=0, keepdims=True)
            idx = jnp.min(jnp.where(gates[u] == best, blk, float(nblk)), axis=0, keepdims=True)
            idx = jnp.where(best > NEG_INF, idx, -1.0)
            sels[u].append(idx)
            gates[u] = jnp.where(blk == idx, NEG_INF, gates[u])

    qs = {u: (q_t[u] * (HEAD_DIM ** -0.5 * LOG2E)).astype(bf16) for u in units}

    def block_scores(u, j):
        t, h = u
        jc = jnp.minimum(j, nblk - 1)
        dist = jnp.clip(curs[t] - jc, 0, N_BIAS_TILES - 1)
        kj = k_ref[pl.ds(pl.multiple_of(jc * MOBA_BLOCK, MOBA_BLOCK), MOBA_BLOCK), hcols[h]]
        return _dot(kj, qs[u]) + bias_ref[h, dist]

    def block_values(u, j):
        jc = jnp.minimum(j, nblk - 1)
        return vt_ref[0, hcols[u[1]], pl.ds(pl.multiple_of(jc * MOBA_BLOCK, MOBA_BLOCK), MOBA_BLOCK)]

    key = lax.broadcasted_iota(jnp.int32, (MOBA_BLOCK, MOBA_BLOCK), 0)
    qry = lax.broadcasted_iota(jnp.int32, (MOBA_BLOCK, MOBA_BLOCK), 1)
    init = []
    for ui, u in enumerate(units):
        s = jnp.where(qry >= key, block_scores(u, curs[u[0]]), NEG_INF)
        m0 = jnp.max(s, axis=0, keepdims=True)
        p = jnp.exp2(s - m0)
        l0 = jnp.sum(p, axis=0, keepdims=True)
        p_scr[ui, 0] = p.astype(bf16)
        init.append((m0, l0, jnp.zeros((HEAD_DIM, MOBA_BLOCK), f32)))

    slots = [(ui, s) for ui in range(len(units)) for s in range(MOBA_UNROLL)]

    def stash_scores(ui, slot, j):
        s = block_scores(units[ui], j)
        s_scr[ui, slot] = s
        smax_scr[ui, slot] = jnp.max(s, axis=0, keepdims=True)

    def pending_pv(ui, it):
        u = units[ui]
        first_block = (it - 1) * MOBA_UNROLL
        pv = _dot(block_values(u, jnp.where(it == 0, curs[u[0]], first_block)), p_scr[ui, 0])
        for s in range(1, MOBA_UNROLL):
            pv = pv + _dot(block_values(u, jnp.maximum(first_block + s, 0)), p_scr[ui, s])
        return pv

    for ui, s in slots:
        stash_scores(ui, s, jnp.int32(s))
        if s > 0:
            p_scr[ui, s] = jnp.zeros((MOBA_BLOCK, MOBA_BLOCK), bf16)

    def softmax_group(ui, base, m_prev, l_prev):
        u = units[ui]
        chosen = []
        for s in range(MOBA_UNROLL):
            j = base + s
            jf = j.astype(f32)
            chosen.append(((sels[u][0] == jf) | (sels[u][1] == jf) | (sels[u][2] == jf))
                          & (j < curs[u[0]]))
        m_new = m_prev
        for s in range(MOBA_UNROLL):
            m_new = jnp.maximum(m_new, jnp.where(chosen[s], smax_scr[ui, s], NEG_INF))
        alpha = jnp.exp2(m_prev - m_new)
        l_new = alpha * l_prev
        ps = []
        for s in range(MOBA_UNROLL):
            p = jnp.exp2(s_scr[ui, s] - jnp.where(chosen[s], m_new, float("inf")))
            l_new = l_new + jnp.sum(p, axis=0, keepdims=True)
            ps.append(p.astype(bf16))
        return m_new, l_new, alpha, ps

    def step(it, state):
        base = it * MOBA_UNROLL
        pv = [pending_pv(ui, it) for ui in range(len(units))]
        out = []
        for ui in range(len(units)):
            m_prev, l_prev, acc = state[ui]
            m_new, l_new, alpha, ps = softmax_group(ui, base, m_prev, l_prev)
            for s in range(MOBA_UNROLL):
                p_scr[ui, s] = ps[s]
            out.append((m_new, l_new, alpha * (acc + pv[ui])))
        for ui, s in slots:
            stash_scores(ui, s, base + MOBA_UNROLL + s)
        return tuple(out)

    n_steps = (curs[-1] + MOBA_UNROLL - 1) // MOBA_UNROLL
    state = lax.fori_loop(0, n_steps - 1, step, tuple(init))
    base = (n_steps - 1) * MOBA_UNROLL
    pv = [pending_pv(ui, n_steps - 1) for ui in range(len(units))]
    last = [softmax_group(ui, base, state[ui][0], state[ui][1]) for ui in range(len(units))]
    for ui, (t, h) in enumerate(units):
        _, l_fin, alpha, ps = last[ui]
        acc = alpha * (state[ui][2] + pv[ui])
        for s in range(MOBA_UNROLL):
            acc = acc + _dot(block_values(units[ui], base + s), ps[s])
        z = z_ref[qrows[t], hcols[h]].astype(f32)
        o_ref[qrows[t], hcols[h]] = ((acc / l_fin).T * (z * jax.nn.sigmoid(z))).astype(bf16)


def _moba_attention(q, z, k, v_t, k_mean, bias, bsz, seq, name):
    m = bsz * seq
    nq = seq // MOBA_BLOCK
    npair = nq // MOBA_QBLOCKS
    hp = MOBA_HEADS_PER_STEP
    w = hp * HEAD_DIM
    n_units = MOBA_QBLOCKS * hp
    tile = pl.BlockSpec((MOBA_QBLOCKS * MOBA_BLOCK, w), lambda b, g, p: (b * npair + p, g))
    return pl.pallas_call(
        _moba_kernel,
        grid=(bsz, HEADS // hp, npair),
        in_specs=[
            tile, tile,
            pl.BlockSpec((seq, w), lambda b, g, p: (b, g)),
            pl.BlockSpec((1, w, seq), lambda b, g, p: (b, g, 0)),
            pl.BlockSpec((1, nq, w), lambda b, g, p: (b, 0, g)),
            pl.BlockSpec((hp, N_BIAS_TILES, MOBA_BLOCK, MOBA_BLOCK), lambda b, g, p: (g, 0, 0, 0)),
        ],
        out_specs=tile,
        out_shape=jax.ShapeDtypeStruct((m, KEY_W), bf16),
        scratch_shapes=[pltpu.VMEM((n_units, MOBA_UNROLL, MOBA_BLOCK, MOBA_BLOCK), f32),
                        pltpu.VMEM((n_units, MOBA_UNROLL, 1, MOBA_BLOCK), f32),
                        pltpu.VMEM((n_units, MOBA_UNROLL, MOBA_BLOCK, MOBA_BLOCK), bf16)],
        compiler_params=_cparams(("parallel", "parallel", "arbitrary")),
        name=name,
    )(q, z, k, v_t, k_mean, bias)


def kernel(x, a_norm_g, a_w_in, a_conv_w, a_log, a_dt_bias, a_out_norm_g, a_w_out, kv_norm_g, w_kv, b_norm_g, b_w_in, b_w_out, rel_bias, final_norm_g):
    bsz, seq, d = x.shape
    m = bsz * seq
    assert d == D_MODEL and seq % MOBA_BLOCK == 0 and seq % GDN_TB == 0 and m % MM_TM == 0 and m % GATES_TM == 0
    xf = x.reshape(m, d).astype(f32)

    qkvz_w = CONV_CH + KEY_W
    for i in range(a_w_in.shape[0]):
        w_ab = jnp.pad(a_w_in[i, :, qkvz_w:], ((0, 0), (0, LANES - 2 * HEADS)))
        proj, ab = _gdn_in_proj(xf, a_norm_g[i], a_w_in, i, qkvz_w, w_ab, f"gdn{i}_in_proj")
        cols = _gdn_gates(ab, a_log[i], a_dt_bias[i])
        o = _gdn_core(proj, a_conv_w[i].astype(f32), cols, a_out_norm_g[i].astype(f32),
                      bsz, seq, f"gdn{i}_core")
        xf = _out_proj(o, a_w_out, i, xf, final_norm_g, False, f"gdn{i}_out_proj")

    k, v_t, k_mean = _kv_proj(xf, kv_norm_g, w_kv, bsz, seq)
    k_mean = k_mean.reshape(bsz, seq // MOBA_BLOCK, KEY_W)
    bias = _bias_tiles(rel_bias)

    n_b = b_w_in.shape[0]
    for j in range(n_b):
        q, z = _moba_in_proj(xf, b_norm_g[j], b_w_in, j, f"moba{j}_in_proj")
        o = _moba_attention(q, z, k, v_t, k_mean, bias, bsz, seq, f"moba{j}_attn")
        xf = _out_proj(o, b_w_out, j, xf, final_norm_g, j == n_b - 1, f"moba{j}_out_proj")
    return xf.reshape(bsz, seq, d).astype(x.dtype)
```

```python
import functools
import math

import jax
import jax.numpy as jnp
from jax import lax
from jax.experimental import pallas as pl
from jax.experimental.pallas import tpu as pltpu

f32 = jnp.float32
bf16 = jnp.bfloat16

D_MODEL = 1024
HEADS = 8
HEAD_DIM = 128
KEY_W = HEADS * HEAD_DIM
CONV_CH = 3 * KEY_W
CONV_W = 4
GDN_IN_W = CONV_CH + KEY_W + 2 * HEADS
CHUNK = 64
MOBA_BLOCK = 256
MOBA_TOPK = 3
N_BUCKETS = 32
MAX_DIST = 2048
EPS = 1e-6
NEG_INF = float("-inf")
LOG2E = math.log2(math.e)

LANES = 128
SUBLANES = 8
VMEM_LIMIT_BYTES = 56 * 1024 * 1024

MM_TM = 1024
GDN_TN = 2048
MOBA_IN_TM = 512
KV_TM = 512
GATES_TM = 2048
GDN_TB = 256
GDN_GROUP = 4
CONV_BLOCK = 256
MOBA_UNROLL = 4
MOBA_HEADS_PER_STEP = 2
MOBA_QBLOCKS = 4
N_BIAS_TILES = MAX_DIST // MOBA_BLOCK + 2


def _cparams(sem):
    return pltpu.CompilerParams(dimension_semantics=sem, vmem_limit_bytes=VMEM_LIMIT_BYTES)


def _dot(a, b):
    return jnp.dot(a, b, preferred_element_type=f32)


def _dot_nt(a, b):
    return lax.dot_general(a, b, (((1,), (1,)), ((), ())), preferred_element_type=f32)


def _dot_tn(a, b):
    return lax.dot_general(a, b, (((0,), (0,)), ((), ())), preferred_element_type=f32)


def _split(x):
    hi = x.astype(bf16)
    lo = (x - hi.astype(f32)).astype(bf16)
    return hi, lo


def _rmsnorm_bf16(x_ref, g_ref, rows=slice(None)):
    x = x_ref[rows, :]
    y = x * lax.rsqrt(jnp.mean(x * x, axis=-1, keepdims=True) + EPS)
    return (y * g_ref[...]).astype(bf16)


def _cast_weights_once(step, w_ref, wb_ref):
    @pl.when(step == 0)
    def _():
        wb_ref[...] = w_ref[...].astype(bf16)


def _gdn_in_kernel(x_ref, g_ref, w_ref, ws_ref, o_ref, os_ref, wb_ref):
    _cast_weights_once(pl.program_id(1), w_ref, wb_ref)
    o_ref[...] = _dot(_rmsnorm_bf16(x_ref, g_ref), wb_ref[...]).astype(o_ref.dtype)
    side_rows = os_ref.shape[0]
    rows = pl.ds(pl.multiple_of(pl.program_id(0) * side_rows, side_rows), side_rows)
    os_ref[...] = _dot(_rmsnorm_bf16(x_ref, g_ref, rows), ws_ref[...].astype(bf16))


def _gdn_in_proj(x, g, w_stack, layer, n, w_side, name):
    m, k = x.shape
    ns = w_side.shape[1]
    n_col = n // GDN_TN
    side_rows = MM_TM // n_col
    return pl.pallas_call(
        _gdn_in_kernel,
        grid=(n_col, m // MM_TM),
        in_specs=[
            pl.BlockSpec((MM_TM, k), lambda j, i: (i, 0)),
            pl.BlockSpec((1, k), lambda j, i: (0, 0)),
            pl.BlockSpec((None, k, GDN_TN), lambda j, i: (layer, 0, j)),
            pl.BlockSpec((k, ns), lambda j, i: (0, 0)),
        ],
        out_specs=[
            pl.BlockSpec((MM_TM, GDN_TN), lambda j, i: (i, j)),
            pl.BlockSpec((side_rows, ns), lambda j, i: (i * n_col + j, 0)),
        ],
        out_shape=[jax.ShapeDtypeStruct((m, n), bf16), jax.ShapeDtypeStruct((m, ns), f32)],
        scratch_shapes=[pltpu.VMEM((k, GDN_TN), bf16)],
        compiler_params=_cparams(("arbitrary", "arbitrary")),
        name=name,
    )(x, g.reshape(1, k).astype(f32), w_stack, w_side)


def _moba_in_kernel(x_ref, g_ref, w_ref, q_ref, z_ref, wb_ref):
    _cast_weights_once(pl.program_id(0), w_ref, wb_ref)
    xn = _rmsnorm_bf16(x_ref, g_ref)
    q_ref[...] = _dot(xn, wb_ref[:, :KEY_W])
    z_ref[...] = _dot(xn, wb_ref[:, KEY_W:]).astype(bf16)


def _moba_in_proj(x, g, w_stack, layer, name):
    m, k = x.shape
    return pl.pallas_call(
        _moba_in_kernel,
        grid=(m // MOBA_IN_TM,),
        in_specs=[
            pl.BlockSpec((MOBA_IN_TM, k), lambda i: (i, 0)),
            pl.BlockSpec((1, k), lambda i: (0, 0)),
            pl.BlockSpec((None, k, 2 * KEY_W), lambda i: (layer, 0, 0)),
        ],
        out_specs=[
            pl.BlockSpec((MOBA_IN_TM, KEY_W), lambda i: (i, 0)),
            pl.BlockSpec((MOBA_IN_TM, KEY_W), lambda i: (i, 0)),
        ],
        out_shape=[jax.ShapeDtypeStruct((m, KEY_W), f32), jax.ShapeDtypeStruct((m, KEY_W), bf16)],
        scratch_shapes=[pltpu.VMEM((k, 2 * KEY_W), bf16)],
        compiler_params=_cparams(("arbitrary",)),
        name=name,
    )(x, g.reshape(1, k).astype(f32), w_stack)


def _kv_kernel(x_ref, g_ref, w_ref, k_ref, vt_ref, km_ref, wb_ref):
    _cast_weights_once(pl.program_id(0), w_ref, wb_ref)
    acc = _dot(_rmsnorm_bf16(x_ref, g_ref), wb_ref[...])
    k = acc[:, :KEY_W]
    k_ref[...] = k.astype(bf16)
    for blk in range(KV_TM // MOBA_BLOCK):
        km_ref[blk] = jnp.mean(k[blk * MOBA_BLOCK:(blk + 1) * MOBA_BLOCK], axis=0, keepdims=True)
    vt_ref[0] = acc[:, KEY_W:].T.astype(bf16)


def _kv_proj(x, g, w, bsz, seq):
    m, k = x.shape
    nt = seq // KV_TM
    blocks_per_tile = KV_TM // MOBA_BLOCK
    return pl.pallas_call(
        _kv_kernel,
        grid=(m // KV_TM,),
        in_specs=[
            pl.BlockSpec((KV_TM, k), lambda i: (i, 0)),
            pl.BlockSpec((1, k), lambda i: (0, 0)),
            pl.BlockSpec((k, 2 * KEY_W), lambda i: (0, 0)),
        ],
        out_specs=[
            pl.BlockSpec((KV_TM, KEY_W), lambda i: (i, 0)),
            pl.BlockSpec((1, KEY_W, KV_TM), lambda i: (i // nt, 0, i % nt)),
            pl.BlockSpec((blocks_per_tile, 1, KEY_W), lambda i: (i, 0, 0)),
        ],
        out_shape=[
            jax.ShapeDtypeStruct((m, KEY_W), bf16),
            jax.ShapeDtypeStruct((bsz, KEY_W, seq), bf16),
            jax.ShapeDtypeStruct((m // MOBA_BLOCK, 1, KEY_W), f32),
        ],
        scratch_shapes=[pltpu.VMEM((k, 2 * KEY_W), bf16)],
        compiler_params=_cparams(("arbitrary",)),
        name="kv_proj",
    )(x, g.reshape(1, k).astype(f32), w)


def _out_kernel(a_ref, w_ref, r_ref, g_ref, o_ref, wb_ref, *, final_norm):
    _cast_weights_once(pl.program_id(0), w_ref, wb_ref)
    y = r_ref[...] + _dot(a_ref[...], wb_ref[...])
    if final_norm:
        y = y * lax.rsqrt(jnp.mean(y * y, axis=-1, keepdims=True) + EPS) * g_ref[...]
    o_ref[...] = y


def _out_proj(a, w_stack, layer, res, g, final_norm, name):
    m, k = a.shape
    n = w_stack.shape[2]
    return pl.pallas_call(
        functools.partial(_out_kernel, final_norm=final_norm),
        grid=(m // MM_TM,),
        in_specs=[
            pl.BlockSpec((MM_TM, k), lambda i: (i, 0)),
            pl.BlockSpec((None, k, n), lambda i: (layer, 0, 0)),
            pl.BlockSpec((MM_TM, n), lambda i: (i, 0)),
            pl.BlockSpec((1, n), lambda i: (0, 0)),
        ],
        out_specs=pl.BlockSpec((MM_TM, n), lambda i: (i, 0)),
        out_shape=jax.ShapeDtypeStruct((m, n), f32),
        scratch_shapes=[pltpu.VMEM((k, n), bf16)],
        compiler_params=_cparams(("arbitrary",)),
        name=name,
    )(a, w_stack, res, g.reshape(1, n).astype(f32))


def _gates_kernel(ab_ref, alog_ref, dt_ref, o_ref):
    ab = ab_ref[...]
    rows = ab.shape[0]
    x = ab + dt_ref[...]
    softplus = jnp.maximum(x, 0.0) + jnp.log1p(jnp.exp(-jnp.abs(x)))
    g = -jnp.exp(alog_ref[...]) * softplus
    pos = lax.broadcasted_iota(jnp.int32, ab.shape, 0) % CHUNK
    fwd = g
    bwd = g
    s = 1
    while s < CHUNK:
        fwd = fwd + jnp.where(pos >= s, pltpu.roll(fwd, s, 0), 0.0)
        bwd = bwd + jnp.where(pos < CHUNK - s, pltpu.roll(bwd, rows - s, 0), 0.0)
        s *= 2
    lane = lax.broadcasted_iota(jnp.int32, ab.shape, 1)
    out = jnp.where(lane < HEADS, fwd, jax.nn.sigmoid(ab))
    out = jnp.where(lane < 2 * HEADS, out, pltpu.roll(jnp.exp(fwd), 2 * HEADS, 1))
    out = jnp.where(lane < 3 * HEADS, out, pltpu.roll(jnp.exp(bwd - g), 3 * HEADS, 1))
    o_ref[...] = jnp.where(lane < 4 * HEADS, out, 0.0)


def _gdn_gates(ab, a_log, dt_bias):
    m = ab.shape[0]
    lane_pad = (0, LANES - HEADS)
    alog = jnp.pad(a_log.astype(f32), lane_pad).reshape(1, LANES)
    dtb = jnp.pad(dt_bias.astype(f32), lane_pad).reshape(1, LANES)
    tile = pl.BlockSpec((GATES_TM, LANES), lambda i: (i, 0))
    vec = pl.BlockSpec((1, LANES), lambda i: (0, 0))
    return pl.pallas_call(
        _gates_kernel,
        grid=(m // GATES_TM,),
        in_specs=[tile, vec, vec],
        out_specs=tile,
        out_shape=jax.ShapeDtypeStruct((m, LANES), f32),
        compiler_params=_cparams(("parallel",)),
        name="gdn_gates",
    )(ab, alog, dtb)


def _neumann_inverse(mats):
    n = mats[0].shape[0]
    row = lax.broadcasted_iota(jnp.int32, (n, n), 0)
    col = lax.broadcasted_iota(jnp.int32, (n, n), 1)
    eye = jnp.where(row == col, 1.0, 0.0).astype(f32)

    ts = [eye - a for a in mats]
    xbs = [a.astype(bf16) for a in mats]
    p = 1
    while True:
        xbs = [_dot(xb, xb).astype(bf16) for xb in xbs]
        p *= 2
        ts = [t + _dot(t.astype(bf16), xb) for t, xb in zip(ts, xbs)]
        if 2 * p >= n:
            return ts


def _gdn_kernel(x_ref, z_ref, cw_ref, cols_ref, og_ref, o_ref,
                halo, qkvn, state):
    tb = x_ref.shape[0]
    t = pl.program_id(1)

    @pl.when(t == 0)
    def _():
        halo[...] = jnp.zeros_like(halo)
        state[...] = jnp.zeros_like(state)

    trow = lax.broadcasted_iota(jnp.int32, (tb, tb), 0)
    tcol = lax.broadcasted_iota(jnp.int32, (tb, tb), 1)
    shifts = [jnp.where(trow - tcol == s, 1.0, 0.0).astype(bf16) for s in range(1, CONV_W)]
    hrow = lax.broadcasted_iota(jnp.int32, (SUBLANES, CONV_BLOCK), 0)
    for cb in range(CONV_CH // CONV_BLOCK):
        cs = slice(cb * CONV_BLOCK, (cb + 1) * CONV_BLOCK)
        xb = x_ref[:, cs]
        acc = xb.astype(f32) * cw_ref[CONV_W - 1:CONV_W, cs]
        patch = jnp.zeros((SUBLANES, CONV_BLOCK), f32)
        for s in range(1, CONV_W):
            w_s = cw_ref[CONV_W - 1 - s:CONV_W - s, cs]
            acc = acc + _dot(shifts[s - 1], xb) * w_s
            patch = patch + jnp.where(hrow < s, pltpu.roll(halo[:, cs], s, 0), 0.0) * w_s
        acc = jnp.concatenate([acc[:SUBLANES] + patch, acc[SUBLANES:]], axis=0)
        y = acc * jax.nn.sigmoid(acc)
        for half in range(CONV_BLOCK // LANES):
            lane_block = cb * (CONV_BLOCK // LANES) + half
            yh = y[:, half * LANES:(half + 1) * LANES]
            if lane_block < 2 * HEADS:
                yh = yh * lax.rsqrt(jnp.sum(yh * yh, axis=-1, keepdims=True) + EPS)
                if lane_block < HEADS:
                    yh = yh * (HEAD_DIM ** -0.5)
            qkvn[:, lane_block * LANES:(lane_block + 1) * LANES] = yh
    halo[...] = x_ref[tb - 2 * SUBLANES:tb, :].astype(f32)[SUBLANES:]

    row = lax.broadcasted_iota(jnp.int32, (CHUNK, CHUNK), 0)
    col = lax.broadcasted_iota(jnp.int32, (CHUNK, CHUNK), 1)
    tril = row >= col
    strict = row > col
    og = og_ref[...]

    hs = range(HEADS)
    qcols = [slice(h * HEAD_DIM, (h + 1) * HEAD_DIM) for h in hs]

    def group_step(gi, carry):
        items = [(ci, h) for ci in range(GDN_GROUP) for h in hs]
        rows, g_rows, ctile = [], [], []
        for ci in range(GDN_GROUP):
            c = gi * GDN_GROUP + ci
            rows.append(pl.ds(pl.multiple_of(c * CHUNK, CHUNK), CHUNK))
            ctile.append(cols_ref[0, c])
            g_rows.append(ctile[ci].T)

        def col(ci, h, which):
            return ctile[ci][:, which * HEADS + h:which * HEADS + h + 1]

        q = {it: qkvn[rows[it[0]], qcols[it[1]]] for it in items}
        k = {(ci, h): qkvn[rows[ci], KEY_W + h * HEAD_DIM:KEY_W + (h + 1) * HEAD_DIM] for ci, h in items}
        v = {(ci, h): qkvn[rows[ci], 2 * KEY_W + h * HEAD_DIM:2 * KEY_W + (h + 1) * HEAD_DIM]
             for ci, h in items}
        decay = {(ci, h): jnp.exp(jnp.where(tril, col(ci, h, 0) - g_rows[ci][h:h + 1, :], NEG_INF))
                 for ci, h in items}
        k_beta = {it: k[it] * col(*it, 1) for it in items}
        kq = {it: _dot_nt(jnp.concatenate([k_beta[it], q[it]], axis=0).astype(bf16), k[it].astype(bf16))
              for it in items}
        t_inv = dict(zip(items, _neumann_inverse(
            [jnp.where(strict, kq[it][:CHUNK] * decay[it], 0.0) for it in items])))
        uw = {it: _dot(t_inv[it].astype(bf16),
                       jnp.concatenate([v[it] * col(*it, 1), k_beta[it] * col(*it, 2)], axis=1).astype(bf16))
              for it in items}
        w_qd = {it: jnp.concatenate([uw[it][:, HEAD_DIM:], q[it] * col(*it, 2)], axis=0).astype(bf16)
                for it in items}
        qk = {it: (kq[it][CHUNK:] * decay[it]).astype(bf16) for it in items}
        k_dec = {it: (k[it] * col(*it, 3)).astype(bf16) for it in items}

        for ci in range(GDN_GROUP):
            s_prev = [state[h] for h in hs]
            ws_qs = [_dot(w_qd[ci, h], s_prev[h].astype(bf16)) for h in hs]
            v_new = [(uw[ci, h][:, :HEAD_DIM] - ws_qs[h][:CHUNK]).astype(bf16) for h in hs]
            o = [ws_qs[h][CHUNK:] + _dot(qk[ci, h], v_new[h]) for h in hs]
            for h in hs:
                g_last = col(ci, h, 2)[CHUNK - 1:CHUNK, :]
                state[h] = s_prev[h] * g_last + _dot_tn(k_dec[ci, h], v_new[h])
            for h in hs:
                on = o[h] * lax.rsqrt(jnp.mean(o[h] * o[h], axis=-1, keepdims=True) + EPS) * og
                z = z_ref[rows[ci], qcols[h]].astype(f32)
                o_ref[rows[ci], qcols[h]] = (on * (z * jax.nn.sigmoid(z))).astype(bf16)
        return carry

    lax.fori_loop(0, tb // (CHUNK * GDN_GROUP), group_step, 0)


def _gdn_core(proj, conv_w, cols, out_norm_g, bsz, seq, name):
    m = bsz * seq
    nt = seq // GDN_TB
    cpt = GDN_TB // CHUNK
    return pl.pallas_call(
        _gdn_kernel,
        grid=(bsz, nt),
        in_specs=[
            pl.BlockSpec((GDN_TB, CONV_CH), lambda b, t: (b * nt + t, 0)),
            pl.BlockSpec((GDN_TB, KEY_W), lambda b, t: (b * nt + t, CONV_CH // KEY_W)),
            pl.BlockSpec((CONV_W, CONV_CH), lambda b, t: (0, 0)),
            pl.BlockSpec((1, cpt, CHUNK, LANES), lambda b, t: (b, t, 0, 0)),
            pl.BlockSpec((1, HEAD_DIM), lambda b, t: (0, 0)),
        ],
        out_specs=pl.BlockSpec((GDN_TB, KEY_W), lambda b, t: (b * nt + t, 0)),
        out_shape=jax.ShapeDtypeStruct((m, KEY_W), bf16),
        scratch_shapes=[
            pltpu.VMEM((SUBLANES, CONV_CH), f32),
            pltpu.VMEM((GDN_TB, CONV_CH), f32),
            pltpu.VMEM((HEADS, HEAD_DIM, HEAD_DIM), f32),
        ],
        compiler_params=_cparams(("parallel", "arbitrary")),
        name=name,
    )(proj, proj, conv_w, cols.reshape(bsz, seq // CHUNK, CHUNK, LANES), out_norm_g.reshape(1, HEAD_DIM))


def _t5_bucket_host(n):
    max_exact = N_BUCKETS // 2
    if n < max_exact:
        return n
    large = max_exact + int(math.log(n / max_exact) / math.log(MAX_DIST / max_exact) * (N_BUCKETS - max_exact))
    return min(large, N_BUCKETS - 1)


def _bias_kernel(rb_ref, o_ref):
    h = pl.program_id(0)
    key = lax.broadcasted_iota(jnp.int32, (MOBA_BLOCK, MOBA_BLOCK), 0)
    qry = lax.broadcasted_iota(jnp.int32, (MOBA_BLOCK, MOBA_BLOCK), 1)
    max_exact = N_BUCKETS // 2
    for d in range(N_BIAS_TILES):
        n = jnp.maximum(d * MOBA_BLOCK + qry - key, 0)
        nf = jnp.maximum(n, 1).astype(f32)
        large = max_exact + (jnp.log(nf / max_exact) / math.log(MAX_DIST / max_exact)
                             * (N_BUCKETS - max_exact)).astype(jnp.int32)
        large = jnp.minimum(large, N_BUCKETS - 1)
        bucket = jnp.where(n < max_exact, n, large)
        b_lo = max(_t5_bucket_host(max(d * MOBA_BLOCK - (MOBA_BLOCK - 1), 0)) - 1, 0)
        b_hi = min(_t5_bucket_host(d * MOBA_BLOCK + MOBA_BLOCK - 1) + 1, N_BUCKETS - 1)
        out = jnp.zeros((MOBA_BLOCK, MOBA_BLOCK), f32)
        for b in range(b_lo, b_hi + 1):
            out = jnp.where(bucket == b, rb_ref[h, b], out)
        o_ref[0, d] = out * LOG2E


def _bias_tiles(rel_bias):
    return pl.pallas_call(
        _bias_kernel,
        grid=(HEADS,),
        in_specs=[pl.BlockSpec(memory_space=pltpu.SMEM)],
        out_specs=pl.BlockSpec((1, N_BIAS_TILES, MOBA_BLOCK, MOBA_BLOCK), lambda h: (h, 0, 0, 0)),
        out_shape=jax.ShapeDtypeStruct((HEADS, N_BIAS_TILES, MOBA_BLOCK, MOBA_BLOCK), f32),
        compiler_params=_cparams(("parallel",)),
        name="t5_bias_tiles",
    )(rel_bias.T.astype(f32))


def _moba_kernel(q_ref, z_ref, k_ref, vt_ref, km_ref, bias_ref, o_ref, s_scr, smax_scr, p_scr):
    pair = pl.program_id(2)
    nblk = km_ref.shape[1]
    curs = [MOBA_QBLOCKS * pair + t for t in range(MOBA_QBLOCKS)]
    qrows = [slice(t * MOBA_BLOCK, (t + 1) * MOBA_BLOCK) for t in range(MOBA_QBLOCKS)]
    units = [(t, h) for t in range(MOBA_QBLOCKS) for h in range(MOBA_HEADS_PER_STEP)]
    hcols = [slice(h * HEAD_DIM, (h + 1) * HEAD_DIM) for h in range(MOBA_HEADS_PER_STEP)]
    q_t = {(t, h): q_ref[qrows[t], hcols[h]].T for t, h in units}

    blk = lax.broadcasted_iota(jnp.int32, (nblk, MOBA_BLOCK), 0).astype(f32)
    gates = {}
    for t, h in units:
        qh, ql = _split(q_t[t, h])
        kmh, kml = _split(km_ref[0, :, hcols[h]])
        gate = _dot(kmh, qh) + _dot(kml, qh) + _dot(kmh, ql)
        gates[t, h] = jnp.where(blk < curs[t].astype(f32), gate, NEG_INF)
    sels = {u: [] for u in units}
    for _ in range(MOBA_TOPK):
        for u in units:
            best = jnp.max(gates[u], axis=0, keepdims=True)
            idx = jnp.min(jnp.where(gates[u] == best, blk, float(nblk)), axis=0, keepdims=True)
            idx = jnp.where(best > NEG_INF, idx, -1.0)
            sels[u].append(idx)
            gates[u] = jnp.where(blk == idx, NEG_INF, gates[u])

    qs = {u: (q_t[u] * (HEAD_DIM ** -0.5 * LOG2E)).astype(bf16) for u in units}

    def block_scores(u, j):
        t, h = u
        jc = jnp.minimum(j, nblk - 1)
        dist = jnp.clip(curs[t] - jc, 0, N_BIAS_TILES - 1)
        kj = k_ref[pl.ds(pl.multiple_of(jc * MOBA_BLOCK, MOBA_BLOCK), MOBA_BLOCK), hcols[h]]
        return _dot(kj, qs[u]) + bias_ref[h, dist]

    def block_values(u, j):
        jc = jnp.minimum(j, nblk - 1)
        return vt_ref[0, hcols[u[1]], pl.ds(pl.multiple_of(jc * MOBA_BLOCK, MOBA_BLOCK), MOBA_BLOCK)]

    key = lax.broadcasted_iota(jnp.int32, (MOBA_BLOCK, MOBA_BLOCK), 0)
    qry = lax.broadcasted_iota(jnp.int32, (MOBA_BLOCK, MOBA_BLOCK), 1)
    init = []
    for ui, u in enumerate(units):
        s = jnp.where(qry >= key, block_scores(u, curs[u[0]]), NEG_INF)
        m0 = jnp.max(s, axis=0, keepdims=True)
        p = jnp.exp2(s - m0)
        l0 = jnp.sum(p, axis=0, keepdims=True)
        p_scr[ui, 0] = p.astype(bf16)
        init.append((m0, l0, jnp.zeros((HEAD_DIM, MOBA_BLOCK), f32)))

    slots = [(ui, s) for ui in range(len(units)) for s in range(MOBA_UNROLL)]

    def stash_scores(ui, slot, j):
        s = block_scores(units[ui], j)
        s_scr[ui, slot] = s
        smax_scr[ui, slot] = jnp.max(s, axis=0, keepdims=True)

    def pending_pv(ui, it):
        u = units[ui]
        first_block = (it - 1) * MOBA_UNROLL
        pv = _dot(block_values(u, jnp.where(it == 0, curs[u[0]], first_block)), p_scr[ui, 0])
        for s in range(1, MOBA_UNROLL):
            pv = pv + _dot(block_values(u, jnp.maximum(first_block + s, 0)), p_scr[ui, s])
        return pv

    for ui, s in slots:
        stash_scores(ui, s, jnp.int32(s))
        if s > 0:
            p_scr[ui, s] = jnp.zeros((MOBA_BLOCK, MOBA_BLOCK), bf16)

    def softmax_group(ui, base, m_prev, l_prev, n_slots=MOBA_UNROLL):
        u = units[ui]
        chosen = []
        for s in range(n_slots):
            jf = (base + s).astype(f32)
            chosen.append((sels[u][0] == jf) | (sels[u][1] == jf) | (sels[u][2] == jf))
        m_new = m_prev
        for s in range(n_slots):
            m_new = jnp.maximum(m_new, jnp.where(chosen[s], smax_scr[ui, s], NEG_INF))
        alpha = jnp.exp2(m_prev - m_new)
        l_new = alpha * l_prev
        ps = []
        for s in range(n_slots):
            p = jnp.exp2(s_scr[ui, s] - jnp.where(chosen[s], m_new, float("inf")))
            l_new = l_new + jnp.sum(p, axis=0, keepdims=True)
            ps.append(p.astype(bf16))
        return m_new, l_new, alpha, ps

    def step(it, state):
        base = it * MOBA_UNROLL
        pv = [pending_pv(ui, it) for ui in range(len(units))]
        out = []
        for ui in range(len(units)):
            m_prev, l_prev, acc = state[ui]
            m_new, l_new, alpha, ps = softmax_group(ui, base, m_prev, l_prev)
            for s in range(MOBA_UNROLL):
                p_scr[ui, s] = ps[s]
            out.append((m_new, l_new, alpha * (acc + pv[ui])))
        for ui, s in slots:
            stash_scores(ui, s, base + MOBA_UNROLL + s)
        return tuple(out)

    assert MOBA_QBLOCKS == MOBA_UNROLL
    n_steps = pair + 1
    state = lax.fori_loop(0, n_steps - 1, step, tuple(init))
    base = (n_steps - 1) * MOBA_UNROLL
    pv = [pending_pv(ui, n_steps - 1) for ui in range(len(units))]
    last = [softmax_group(ui, base, state[ui][0], state[ui][1], n_slots=t) for ui, (t, h) in enumerate(units)]
    for ui, (t, h) in enumerate(units):
        _, l_fin, alpha, ps = last[ui]
        acc = alpha * (state[ui][2] + pv[ui])
        for s in range(t):
            acc = acc + _dot(block_values(units[ui], base + s), ps[s])
        z = z_ref[qrows[t], hcols[h]].astype(f32)
        o_ref[qrows[t], hcols[h]] = ((acc / l_fin).T * (z * jax.nn.sigmoid(z))).astype(bf16)


def _moba_attention(q, z, k, v_t, k_mean, bias, bsz, seq, name):
    m = bsz * seq
    nq = seq // MOBA_BLOCK
    npair = nq // MOBA_QBLOCKS
    hp = MOBA_HEADS_PER_STEP
    w = hp * HEAD_DIM
    n_units = MOBA_QBLOCKS * hp
    tile = pl.BlockSpec((MOBA_QBLOCKS * MOBA_BLOCK, w), lambda b, g, p: (b * npair + p, g))
    return pl.pallas_call(
        _moba_kernel,
        grid=(bsz, HEADS // hp, npair),
        in_specs=[
            tile, tile,
            pl.BlockSpec((seq, w), lambda b, g, p: (b, g)),
            pl.BlockSpec((1, w, seq), lambda b, g, p: (b, g, 0)),
            pl.BlockSpec((1, nq, w), lambda b, g, p: (b, 0, g)),
            pl.BlockSpec((hp, N_BIAS_TILES, MOBA_BLOCK, MOBA_BLOCK), lambda b, g, p: (g, 0, 0, 0)),
        ],
        out_specs=tile,
        out_shape=jax.ShapeDtypeStruct((m, KEY_W), bf16),
        scratch_shapes=[pltpu.VMEM((n_units, MOBA_UNROLL, MOBA_BLOCK, MOBA_BLOCK), f32),
                        pltpu.VMEM((n_units, MOBA_UNROLL, 1, MOBA_BLOCK), f32),
                        pltpu.VMEM((n_units, MOBA_UNROLL, MOBA_BLOCK, MOBA_BLOCK), bf16)],
        compiler_params=_cparams(("parallel", "parallel", "arbitrary")),
        name=name,
    )(q, z, k, v_t, k_mean, bias)


def kernel(x, a_norm_g, a_w_in, a_conv_w, a_log, a_dt_bias, a_out_norm_g, a_w_out, kv_norm_g, w_kv, b_norm_g, b_w_in, b_w_out, rel_bias, final_norm_g):
    bsz, seq, d = x.shape
    m = bsz * seq
    assert d == D_MODEL and seq % MOBA_BLOCK == 0 and seq % GDN_TB == 0 and m % MM_TM == 0 and m % GATES_TM == 0
    xf = x.reshape(m, d).astype(f32)

    qkvz_w = CONV_CH + KEY_W
    for i in range(a_w_in.shape[0]):
        w_ab = jnp.pad(a_w_in[i, :, qkvz_w:], ((0, 0), (0, LANES - 2 * HEADS)))
        proj, ab = _gdn_in_proj(xf, a_norm_g[i], a_w_in, i, qkvz_w, w_ab, f"gdn{i}_in_proj")
        cols = _gdn_gates(ab, a_log[i], a_dt_bias[i])
        o = _gdn_core(proj, a_conv_w[i].astype(f32), cols, a_out_norm_g[i].astype(f32),
                      bsz, seq, f"gdn{i}_core")
        xf = _out_proj(o, a_w_out, i, xf, final_norm_g, False, f"gdn{i}_out_proj")

    k, v_t, k_mean = _kv_proj(xf, kv_norm_g, w_kv, bsz, seq)
    k_mean = k_mean.reshape(bsz, seq // MOBA_BLOCK, KEY_W)
    bias = _bias_tiles(rel_bias)

    n_b = b_w_in.shape[0]
    for j in range(n_b):
        q, z = _moba_in_proj(xf, b_norm_g[j], b_w_in, j, f"moba{j}_in_proj")
        o = _moba_attention(q, z, k, v_t, k_mean, bias, bsz, seq, f"moba{j}_attn")
        xf = _out_proj(o, b_w_out, j, xf, final_norm_g, j == n_b - 1, f"moba{j}_out_proj")
    return xf.reshape(bsz, seq, d).astype(x.dtype)
```

```python
import functools
import math

import jax
import jax.numpy as jnp
from jax import lax
from jax.experimental import pallas as pl
from jax.experimental.pallas import tpu as pltpu

f32 = jnp.float32
bf16 = jnp.bfloat16

D_MODEL = 1024
HEADS = 8
HEAD_DIM = 128
KEY_W = HEADS * HEAD_DIM
CONV_CH = 3 * KEY_W
CONV_W = 4
GDN_IN_W = CONV_CH + KEY_W + 2 * HEADS
CHUNK = 64
MOBA_BLOCK = 256
MOBA_TOPK = 3
N_BUCKETS = 32
MAX_DIST = 2048
EPS = 1e-6
NEG_INF = float("-inf")
LOG2E = math.log2(math.e)

LANES = 128
SUBLANES = 8
VMEM_LIMIT_BYTES = 56 * 1024 * 1024

MM_TM = 1024
GDN_TN = 2048
MOBA_IN_TM = 512
KV_TM = 512
GATES_TM = 2048
GDN_TB = 256
GDN_GROUP = 4
CONV_BLOCK = 256
MOBA_UNROLL = 4
MOBA_HEADS_PER_STEP = 2
MOBA_QBLOCKS = 4
N_BIAS_TILES = MAX_DIST // MOBA_BLOCK + 2


def _cparams(sem):
    return pltpu.CompilerParams(dimension_semantics=sem, vmem_limit_bytes=VMEM_LIMIT_BYTES)


def _dot(a, b):
    return jnp.dot(a, b, preferred_element_type=f32)


def _dot_nt(a, b):
    return lax.dot_general(a, b, (((1,), (1,)), ((), ())), preferred_element_type=f32)


def _dot_tn(a, b):
    return lax.dot_general(a, b, (((0,), (0,)), ((), ())), preferred_element_type=f32)


def _split(x):
    hi = x.astype(bf16)
    lo = (x - hi.astype(f32)).astype(bf16)
    return hi, lo


def _rmsnorm_bf16(x_ref, g_ref, rows=slice(None)):
    x = x_ref[rows, :]
    y = x * lax.rsqrt(jnp.mean(x * x, axis=-1, keepdims=True) + EPS)
    return (y * g_ref[...]).astype(bf16)


def _cast_weights_once(step, w_ref, wb_ref):
    @pl.when(step == 0)
    def _():
        wb_ref[...] = w_ref[...].astype(bf16)


def _gdn_in_kernel(x_ref, g_ref, w_ref, ws_ref, o_ref, os_ref, wb_ref):
    _cast_weights_once(pl.program_id(1), w_ref, wb_ref)
    o_ref[...] = _dot(_rmsnorm_bf16(x_ref, g_ref), wb_ref[...]).astype(o_ref.dtype)
    side_rows = os_ref.shape[0]
    rows = pl.ds(pl.multiple_of(pl.program_id(0) * side_rows, side_rows), side_rows)
    os_ref[...] = _dot(_rmsnorm_bf16(x_ref, g_ref, rows), ws_ref[...].astype(bf16))


def _gdn_in_proj(x, g, w_stack, layer, n, w_side, name):
    m, k = x.shape
    ns = w_side.shape[1]
    n_col = n // GDN_TN
    side_rows = MM_TM // n_col
    return pl.pallas_call(
        _gdn_in_kernel,
        grid=(n_col, m // MM_TM),
        in_specs=[
            pl.BlockSpec((MM_TM, k), lambda j, i: (i, 0)),
            pl.BlockSpec((1, k), lambda j, i: (0, 0)),
            pl.BlockSpec((None, k, GDN_TN), lambda j, i: (layer, 0, j)),
            pl.BlockSpec((k, ns), lambda j, i: (0, 0)),
        ],
        out_specs=[
            pl.BlockSpec((MM_TM, GDN_TN), lambda j, i: (i, j)),
            pl.BlockSpec((side_rows, ns), lambda j, i: (i * n_col + j, 0)),
        ],
        out_shape=[jax.ShapeDtypeStruct((m, n), bf16), jax.ShapeDtypeStruct((m, ns), f32)],
        scratch_shapes=[pltpu.VMEM((k, GDN_TN), bf16)],
        compiler_params=_cparams(("arbitrary", "arbitrary")),
        name=name,
    )(x, g.reshape(1, k).astype(f32), w_stack, w_side)


def _moba_in_kernel(x_ref, g_ref, w_ref, q_ref, z_ref, wb_ref):
    _cast_weights_once(pl.program_id(0), w_ref, wb_ref)
    xn = _rmsnorm_bf16(x_ref, g_ref)
    q_ref[0] = _dot(xn, wb_ref[:, :KEY_W]).T
    z_ref[...] = _dot(xn, wb_ref[:, KEY_W:]).astype(bf16)


def _moba_in_proj(x, g, w_stack, layer, bsz, seq, name):
    m, k = x.shape
    nt = seq // MOBA_IN_TM
    return pl.pallas_call(
        _moba_in_kernel,
        grid=(m // MOBA_IN_TM,),
        in_specs=[
            pl.BlockSpec((MOBA_IN_TM, k), lambda i: (i, 0)),
            pl.BlockSpec((1, k), lambda i: (0, 0)),
            pl.BlockSpec((None, k, 2 * KEY_W), lambda i: (layer, 0, 0)),
        ],
        out_specs=[
            pl.BlockSpec((1, KEY_W, MOBA_IN_TM), lambda i: (i // nt, 0, i % nt)),
            pl.BlockSpec((MOBA_IN_TM, KEY_W), lambda i: (i, 0)),
        ],
        out_shape=[jax.ShapeDtypeStruct((bsz, KEY_W, seq), f32), jax.ShapeDtypeStruct((m, KEY_W), bf16)],
        scratch_shapes=[pltpu.VMEM((k, 2 * KEY_W), bf16)],
        compiler_params=_cparams(("arbitrary",)),
        name=name,
    )(x, g.reshape(1, k).astype(f32), w_stack)


def _kv_kernel(x_ref, g_ref, w_ref, k_ref, vt_ref, km_ref, wb_ref):
    _cast_weights_once(pl.program_id(0), w_ref, wb_ref)
    acc = _dot(_rmsnorm_bf16(x_ref, g_ref), wb_ref[...])
    k = acc[:, :KEY_W]
    k_ref[...] = k.astype(bf16)
    for blk in range(KV_TM // MOBA_BLOCK):
        km_ref[blk] = jnp.mean(k[blk * MOBA_BLOCK:(blk + 1) * MOBA_BLOCK], axis=0, keepdims=True)
    vt_ref[0] = acc[:, KEY_W:].T.astype(bf16)


def _kv_proj(x, g, w, bsz, seq):
    m, k = x.shape
    nt = seq // KV_TM
    blocks_per_tile = KV_TM // MOBA_BLOCK
    return pl.pallas_call(
        _kv_kernel,
        grid=(m // KV_TM,),
        in_specs=[
            pl.BlockSpec((KV_TM, k), lambda i: (i, 0)),
            pl.BlockSpec((1, k), lambda i: (0, 0)),
            pl.BlockSpec((k, 2 * KEY_W), lambda i: (0, 0)),
        ],
        out_specs=[
            pl.BlockSpec((KV_TM, KEY_W), lambda i: (i, 0)),
            pl.BlockSpec((1, KEY_W, KV_TM), lambda i: (i // nt, 0, i % nt)),
            pl.BlockSpec((blocks_per_tile, 1, KEY_W), lambda i: (i, 0, 0)),
        ],
        out_shape=[
            jax.ShapeDtypeStruct((m, KEY_W), bf16),
            jax.ShapeDtypeStruct((bsz, KEY_W, seq), bf16),
            jax.ShapeDtypeStruct((m // MOBA_BLOCK, 1, KEY_W), f32),
        ],
        scratch_shapes=[pltpu.VMEM((k, 2 * KEY_W), bf16)],
        compiler_params=_cparams(("arbitrary",)),
        name="kv_proj",
    )(x, g.reshape(1, k).astype(f32), w)


def _out_kernel(a_ref, w_ref, r_ref, g_ref, o_ref, wb_ref, *, final_norm):
    _cast_weights_once(pl.program_id(0), w_ref, wb_ref)
    y = r_ref[...] + _dot(a_ref[...], wb_ref[...])
    if final_norm:
        y = y * lax.rsqrt(jnp.mean(y * y, axis=-1, keepdims=True) + EPS) * g_ref[...]
    o_ref[...] = y


def _out_proj(a, w_stack, layer, res, g, final_norm, name):
    m, k = a.shape
    n = w_stack.shape[2]
    return pl.pallas_call(
        functools.partial(_out_kernel, final_norm=final_norm),
        grid=(m // MM_TM,),
        in_specs=[
            pl.BlockSpec((MM_TM, k), lambda i: (i, 0)),
            pl.BlockSpec((None, k, n), lambda i: (layer, 0, 0)),
            pl.BlockSpec((MM_TM, n), lambda i: (i, 0)),
            pl.BlockSpec((1, n), lambda i: (0, 0)),
        ],
        out_specs=pl.BlockSpec((MM_TM, n), lambda i: (i, 0)),
        out_shape=jax.ShapeDtypeStruct((m, n), f32),
        scratch_shapes=[pltpu.VMEM((k, n), bf16)],
        compiler_params=_cparams(("arbitrary",)),
        name=name,
    )(a, w_stack, res, g.reshape(1, n).astype(f32))


def _gates_kernel(ab_ref, alog_ref, dt_ref, o_ref):
    ab = ab_ref[...]
    rows = ab.shape[0]
    x = ab + dt_ref[...]
    softplus = jnp.maximum(x, 0.0) + jnp.log1p(jnp.exp(-jnp.abs(x)))
    g = -jnp.exp(alog_ref[...]) * softplus
    pos = lax.broadcasted_iota(jnp.int32, ab.shape, 0) % CHUNK
    fwd = g
    bwd = g
    s = 1
    while s < CHUNK:
        fwd = fwd + jnp.where(pos >= s, pltpu.roll(fwd, s, 0), 0.0)
        bwd = bwd + jnp.where(pos < CHUNK - s, pltpu.roll(bwd, rows - s, 0), 0.0)
        s *= 2
    lane = lax.broadcasted_iota(jnp.int32, ab.shape, 1)
    out = jnp.where(lane < HEADS, fwd, jax.nn.sigmoid(ab))
    out = jnp.where(lane < 2 * HEADS, out, pltpu.roll(jnp.exp(fwd), 2 * HEADS, 1))
    out = jnp.where(lane < 3 * HEADS, out, pltpu.roll(jnp.exp(bwd - g), 3 * HEADS, 1))
    o_ref[...] = jnp.where(lane < 4 * HEADS, out, 0.0)


def _gdn_gates(ab, a_log, dt_bias):
    m = ab.shape[0]
    lane_pad = (0, LANES - HEADS)
    alog = jnp.pad(a_log.astype(f32), lane_pad).reshape(1, LANES)
    dtb = jnp.pad(dt_bias.astype(f32), lane_pad).reshape(1, LANES)
    tile = pl.BlockSpec((GATES_TM, LANES), lambda i: (i, 0))
    vec = pl.BlockSpec((1, LANES), lambda i: (0, 0))
    return pl.pallas_call(
        _gates_kernel,
        grid=(m // GATES_TM,),
        in_specs=[tile, vec, vec],
        out_specs=tile,
        out_shape=jax.ShapeDtypeStruct((m, LANES), f32),
        compiler_params=_cparams(("parallel",)),
        name="gdn_gates",
    )(ab, alog, dtb)


def _neumann_inverse(mats):
    n = mats[0].shape[0]
    row = lax.broadcasted_iota(jnp.int32, (n, n), 0)
    col = lax.broadcasted_iota(jnp.int32, (n, n), 1)
    eye = jnp.where(row == col, 1.0, 0.0).astype(f32)

    ts = [eye - a for a in mats]
    xbs = [a.astype(bf16) for a in mats]
    p = 1
    while True:
        xbs = [_dot(xb, xb).astype(bf16) for xb in xbs]
        p *= 2
        ts = [t + _dot(t.astype(bf16), xb) for t, xb in zip(ts, xbs)]
        if 2 * p >= n:
            return ts


def _gdn_kernel(x_ref, z_ref, cw_ref, cols_ref, og_ref, o_ref,
                halo, qkvn, state):
    tb = x_ref.shape[0]
    t = pl.program_id(1)

    @pl.when(t == 0)
    def _():
        halo[...] = jnp.zeros_like(halo)
        state[...] = jnp.zeros_like(state)

    trow = lax.broadcasted_iota(jnp.int32, (tb, tb), 0)
    tcol = lax.broadcasted_iota(jnp.int32, (tb, tb), 1)
    shifts = [jnp.where(trow - tcol == s, 1.0, 0.0).astype(bf16) for s in range(1, CONV_W)]
    hrow = lax.broadcasted_iota(jnp.int32, (SUBLANES, CONV_BLOCK), 0)
    for cb in range(CONV_CH // CONV_BLOCK):
        cs = slice(cb * CONV_BLOCK, (cb + 1) * CONV_BLOCK)
        xb = x_ref[:, cs]
        acc = xb.astype(f32) * cw_ref[CONV_W - 1:CONV_W, cs]
        patch = jnp.zeros((SUBLANES, CONV_BLOCK), f32)
        for s in range(1, CONV_W):
            w_s = cw_ref[CONV_W - 1 - s:CONV_W - s, cs]
            acc = acc + _dot(shifts[s - 1], xb) * w_s
            patch = patch + jnp.where(hrow < s, pltpu.roll(halo[:, cs], s, 0), 0.0) * w_s
        acc = jnp.concatenate([acc[:SUBLANES] + patch, acc[SUBLANES:]], axis=0)
        y = acc * jax.nn.sigmoid(acc)
        for half in range(CONV_BLOCK // LANES):
            lane_block = cb * (CONV_BLOCK // LANES) + half
            yh = y[:, half * LANES:(half + 1) * LANES]
            if lane_block < 2 * HEADS:
                yh = yh * lax.rsqrt(jnp.sum(yh * yh, axis=-1, keepdims=True) + EPS)
                if lane_block < HEADS:
                    yh = yh * (HEAD_DIM ** -0.5)
            qkvn[:, lane_block * LANES:(lane_block + 1) * LANES] = yh
    halo[...] = x_ref[tb - 2 * SUBLANES:tb, :].astype(f32)[SUBLANES:]

    row = lax.broadcasted_iota(jnp.int32, (CHUNK, CHUNK), 0)
    col = lax.broadcasted_iota(jnp.int32, (CHUNK, CHUNK), 1)
    tril = row >= col
    strict = row > col
    og = og_ref[...]

    hs = range(HEADS)
    qcols = [slice(h * HEAD_DIM, (h + 1) * HEAD_DIM) for h in hs]

    def group_step(gi, carry):
        items = [(ci, h) for ci in range(GDN_GROUP) for h in hs]
        rows, g_rows, ctile = [], [], []
        for ci in range(GDN_GROUP):
            c = gi * GDN_GROUP + ci
            rows.append(pl.ds(pl.multiple_of(c * CHUNK, CHUNK), CHUNK))
            ctile.append(cols_ref[0, c])
            g_rows.append(ctile[ci].T)

        def col(ci, h, which):
            return ctile[ci][:, which * HEADS + h:which * HEADS + h + 1]

        q = {it: qkvn[rows[it[0]], qcols[it[1]]] for it in items}
        k = {(ci, h): qkvn[rows[ci], KEY_W + h * HEAD_DIM:KEY_W + (h + 1) * HEAD_DIM] for ci, h in items}
        v = {(ci, h): qkvn[rows[ci], 2 * KEY_W + h * HEAD_DIM:2 * KEY_W + (h + 1) * HEAD_DIM]
             for ci, h in items}
        decay = {(ci, h): jnp.exp(jnp.where(tril, col(ci, h, 0) - g_rows[ci][h:h + 1, :], NEG_INF))
                 for ci, h in items}
        k_beta = {it: k[it] * col(*it, 1) for it in items}
        kq = {it: _dot_nt(jnp.concatenate([k_beta[it], q[it]], axis=0).astype(bf16), k[it].astype(bf16))
              for it in items}
        t_inv = dict(zip(items, _neumann_inverse(
            [jnp.where(strict, kq[it][:CHUNK] * decay[it], 0.0) for it in items])))
        uw = {it: _dot(t_inv[it].astype(bf16),
                       jnp.concatenate([v[it] * col(*it, 1), k_beta[it] * col(*it, 2)], axis=1).astype(bf16))
              for it in items}
        w_qd = {it: jnp.concatenate([uw[it][:, HEAD_DIM:], q[it] * col(*it, 2)], axis=0).astype(bf16)
                for it in items}
        qk = {it: (kq[it][CHUNK:] * decay[it]).astype(bf16) for it in items}
        k_dec = {it: (k[it] * col(*it, 3)).astype(bf16) for it in items}

        for ci in range(GDN_GROUP):
            s_prev = [state[h] for h in hs]
            ws_qs = [_dot(w_qd[ci, h], s_prev[h].astype(bf16)) for h in hs]
            v_new = [(uw[ci, h][:, :HEAD_DIM] - ws_qs[h][:CHUNK]).astype(bf16) for h in hs]
            o = [ws_qs[h][CHUNK:] + _dot(qk[ci, h], v_new[h]) for h in hs]
            for h in hs:
                g_last = col(ci, h, 2)[CHUNK - 1:CHUNK, :]
                state[h] = s_prev[h] * g_last + _dot_tn(k_dec[ci, h], v_new[h])
            for h in hs:
                on = o[h] * lax.rsqrt(jnp.mean(o[h] * o[h], axis=-1, keepdims=True) + EPS) * og
                z = z_ref[rows[ci], qcols[h]].astype(f32)
                o_ref[rows[ci], qcols[h]] = (on * (z * jax.nn.sigmoid(z))).astype(bf16)
        return carry

    lax.fori_loop(0, tb // (CHUNK * GDN_GROUP), group_step, 0)


def _gdn_core(proj, conv_w, cols, out_norm_g, bsz, seq, name):
    m = bsz * seq
    nt = seq // GDN_TB
    cpt = GDN_TB // CHUNK
    return pl.pallas_call(
        _gdn_kernel,
        grid=(bsz, nt),
        in_specs=[
            pl.BlockSpec((GDN_TB, CONV_CH), lambda b, t: (b * nt + t, 0)),
            pl.BlockSpec((GDN_TB, KEY_W), lambda b, t: (b * nt + t, CONV_CH // KEY_W)),
            pl.BlockSpec((CONV_W, CONV_CH), lambda b, t: (0, 0)),
            pl.BlockSpec((1, cpt, CHUNK, LANES), lambda b, t: (b, t, 0, 0)),
            pl.BlockSpec((1, HEAD_DIM), lambda b, t: (0, 0)),
        ],
        out_specs=pl.BlockSpec((GDN_TB, KEY_W), lambda b, t: (b * nt + t, 0)),
        out_shape=jax.ShapeDtypeStruct((m, KEY_W), bf16),
        scratch_shapes=[
            pltpu.VMEM((SUBLANES, CONV_CH), f32),
            pltpu.VMEM((GDN_TB, CONV_CH), f32),
            pltpu.VMEM((HEADS, HEAD_DIM, HEAD_DIM), f32),
        ],
        compiler_params=_cparams(("parallel", "arbitrary")),
        name=name,
    )(proj, proj, conv_w, cols.reshape(bsz, seq // CHUNK, CHUNK, LANES), out_norm_g.reshape(1, HEAD_DIM))


def _t5_bucket_host(n):
    max_exact = N_BUCKETS // 2
    if n < max_exact:
        return n
    large = max_exact + int(math.log(n / max_exact) / math.log(MAX_DIST / max_exact) * (N_BUCKETS - max_exact))
    return min(large, N_BUCKETS - 1)


def _bias_kernel(rb_ref, o_ref):
    h = pl.program_id(0)
    key = lax.broadcasted_iota(jnp.int32, (MOBA_BLOCK, MOBA_BLOCK), 0)
    qry = lax.broadcasted_iota(jnp.int32, (MOBA_BLOCK, MOBA_BLOCK), 1)
    max_exact = N_BUCKETS // 2
    for d in range(N_BIAS_TILES):
        n = jnp.maximum(d * MOBA_BLOCK + qry - key, 0)
        nf = jnp.maximum(n, 1).astype(f32)
        large = max_exact + (jnp.log(nf / max_exact) / math.log(MAX_DIST / max_exact)
                             * (N_BUCKETS - max_exact)).astype(jnp.int32)
        large = jnp.minimum(large, N_BUCKETS - 1)
        bucket = jnp.where(n < max_exact, n, large)
        b_lo = max(_t5_bucket_host(max(d * MOBA_BLOCK - (MOBA_BLOCK - 1), 0)) - 1, 0)
        b_hi = min(_t5_bucket_host(d * MOBA_BLOCK + MOBA_BLOCK - 1) + 1, N_BUCKETS - 1)
        out = jnp.zeros((MOBA_BLOCK, MOBA_BLOCK), f32)
        for b in range(b_lo, b_hi + 1):
            out = jnp.where(bucket == b, rb_ref[h, b], out)
        o_ref[0, d] = out * LOG2E


def _bias_tiles(rel_bias):
    return pl.pallas_call(
        _bias_kernel,
        grid=(HEADS,),
        in_specs=[pl.BlockSpec(memory_space=pltpu.SMEM)],
        out_specs=pl.BlockSpec((1, N_BIAS_TILES, MOBA_BLOCK, MOBA_BLOCK), lambda h: (h, 0, 0, 0)),
        out_shape=jax.ShapeDtypeStruct((HEADS, N_BIAS_TILES, MOBA_BLOCK, MOBA_BLOCK), f32),
        compiler_params=_cparams(("parallel",)),
        name="t5_bias_tiles",
    )(rel_bias.T.astype(f32))


def _moba_kernel(q_ref, z_ref, k_ref, vt_ref, km_ref, bias_ref, o_ref, s_scr, smax_scr, p_scr):
    pair = pl.program_id(2)
    nblk = km_ref.shape[1]
    curs = [MOBA_QBLOCKS * pair + t for t in range(MOBA_QBLOCKS)]
    qrows = [slice(t * MOBA_BLOCK, (t + 1) * MOBA_BLOCK) for t in range(MOBA_QBLOCKS)]
    units = [(t, h) for t in range(MOBA_QBLOCKS) for h in range(MOBA_HEADS_PER_STEP)]
    hcols = [slice(h * HEAD_DIM, (h + 1) * HEAD_DIM) for h in range(MOBA_HEADS_PER_STEP)]
    q_t = {(t, h): q_ref[0, hcols[h], qrows[t]] for t, h in units}

    blk = lax.broadcasted_iota(jnp.int32, (nblk, MOBA_BLOCK), 0).astype(f32)
    gates = {}
    for t, h in units:
        qh, ql = _split(q_t[t, h])
        kmh, kml = _split(km_ref[0, :, hcols[h]])
        gate = _dot(kmh, qh) + _dot(kml, qh) + _dot(kmh, ql)
        gates[t, h] = jnp.where(blk < curs[t].astype(f32), gate, NEG_INF)
    sels = {u: [] for u in units}
    for _ in range(MOBA_TOPK):
        for u in units:
            best = jnp.max(gates[u], axis=0, keepdims=True)
            idx = jnp.min(jnp.where(gates[u] == best, blk, float(nblk)), axis=0, keepdims=True)
            idx = jnp.where(best > NEG_INF, idx, -1.0)
            sels[u].append(idx)
            gates[u] = jnp.where(blk == idx, NEG_INF, gates[u])

    qs = {u: (q_t[u] * (HEAD_DIM ** -0.5 * LOG2E)).astype(bf16) for u in units}

    def block_scores(u, j):
        t, h = u
        jc = jnp.minimum(j, nblk - 1)
        dist = jnp.clip(curs[t] - jc, 0, N_BIAS_TILES - 1)
        kj = k_ref[pl.ds(pl.multiple_of(jc * MOBA_BLOCK, MOBA_BLOCK), MOBA_BLOCK), hcols[h]]
        return _dot(kj, qs[u]) + bias_ref[h, dist]

    def block_values(u, j):
        jc = jnp.minimum(j, nblk - 1)
        return vt_ref[0, hcols[u[1]], pl.ds(pl.multiple_of(jc * MOBA_BLOCK, MOBA_BLOCK), MOBA_BLOCK)]

    key = lax.broadcasted_iota(jnp.int32, (MOBA_BLOCK, MOBA_BLOCK), 0)
    qry = lax.broadcasted_iota(jnp.int32, (MOBA_BLOCK, MOBA_BLOCK), 1)
    init = []
    for ui, u in enumerate(units):
        s = jnp.where(qry >= key, block_scores(u, curs[u[0]]), NEG_INF)
        m0 = jnp.max(s, axis=0, keepdims=True)
        p = jnp.exp2(s - m0)
        l0 = jnp.sum(p, axis=0, keepdims=True)
        p_scr[ui, 0] = p.astype(bf16)
        init.append((m0, l0, jnp.zeros((HEAD_DIM, MOBA_BLOCK), f32)))

    slots = [(ui, s) for ui in range(len(units)) for s in range(MOBA_UNROLL)]

    def stash_scores(ui, slot, j):
        s = block_scores(units[ui], j)
        s_scr[ui, slot] = s
        smax_scr[ui, slot] = jnp.max(s, axis=0, keepdims=True)

    def pending_pv(ui, it):
        u = units[ui]
        first_block = (it - 1) * MOBA_UNROLL
        pv = _dot(block_values(u, jnp.where(it == 0, curs[u[0]], first_block)), p_scr[ui, 0])
        for s in range(1, MOBA_UNROLL):
            pv = pv + _dot(block_values(u, jnp.maximum(first_block + s, 0)), p_scr[ui, s])
        return pv

    for ui, s in slots:
        stash_scores(ui, s, jnp.int32(s))
        if s > 0:
            p_scr[ui, s] = jnp.zeros((MOBA_BLOCK, MOBA_BLOCK), bf16)

    def softmax_group(ui, base, m_prev, l_prev, n_slots=MOBA_UNROLL):
        u = units[ui]
        chosen = []
        for s in range(n_slots):
            jf = (base + s).astype(f32)
            chosen.append((sels[u][0] == jf) | (sels[u][1] == jf) | (sels[u][2] == jf))
        m_new = m_prev
        for s in range(n_slots):
            m_new = jnp.maximum(m_new, jnp.where(chosen[s], smax_scr[ui, s], NEG_INF))
        alpha = jnp.exp2(m_prev - m_new)
        l_new = alpha * l_prev
        ps = []
        for s in range(n_slots):
            p = jnp.exp2(s_scr[ui, s] - jnp.where(chosen[s], m_new, float("inf")))
            l_new = l_new + jnp.sum(p, axis=0, keepdims=True)
            ps.append(p.astype(bf16))
        return m_new, l_new, alpha, ps

    def step(it, state):
        base = it * MOBA_UNROLL
        pv = [pending_pv(ui, it) for ui in range(len(units))]
        out = []
        for ui in range(len(units)):
            m_prev, l_prev, acc = state[ui]
            m_new, l_new, alpha, ps = softmax_group(ui, base, m_prev, l_prev)
            for s in range(MOBA_UNROLL):
                p_scr[ui, s] = ps[s]
            out.append((m_new, l_new, alpha * (acc + pv[ui])))
        for ui, s in slots:
            stash_scores(ui, s, base + MOBA_UNROLL + s)
        return tuple(out)

    assert MOBA_QBLOCKS == MOBA_UNROLL
    n_steps = pair + 1
    state = lax.fori_loop(0, n_steps - 1, step, tuple(init))
    base = (n_steps - 1) * MOBA_UNROLL
    pv = [pending_pv(ui, n_steps - 1) for ui in range(len(units))]
    last = [softmax_group(ui, base, state[ui][0], state[ui][1], n_slots=t) for ui, (t, h) in enumerate(units)]
    for ui, (t, h) in enumerate(units):
        _, l_fin, alpha, ps = last[ui]
        acc = alpha * (state[ui][2] + pv[ui])
        for s in range(t):
            acc = acc + _dot(block_values(units[ui], base + s), ps[s])
        z = z_ref[qrows[t], hcols[h]].astype(f32)
        o_ref[qrows[t], hcols[h]] = ((acc / l_fin).T * (z * jax.nn.sigmoid(z))).astype(bf16)


def _moba_attention(q, z, k, v_t, k_mean, bias, bsz, seq, name):
    m = bsz * seq
    nq = seq // MOBA_BLOCK
    npair = nq // MOBA_QBLOCKS
    hp = MOBA_HEADS_PER_STEP
    w = hp * HEAD_DIM
    n_units = MOBA_QBLOCKS * hp
    tile = pl.BlockSpec((MOBA_QBLOCKS * MOBA_BLOCK, w), lambda b, g, p: (b * npair + p, g))
    return pl.pallas_call(
        _moba_kernel,
        grid=(bsz, HEADS // hp, npair),
        in_specs=[
            pl.BlockSpec((1, w, MOBA_QBLOCKS * MOBA_BLOCK), lambda b, g, p: (b, g, p)), tile,
            pl.BlockSpec((seq, w), lambda b, g, p: (b, g)),
            pl.BlockSpec((1, w, seq), lambda b, g, p: (b, g, 0)),
            pl.BlockSpec((1, nq, w), lambda b, g, p: (b, 0, g)),
            pl.BlockSpec((hp, N_BIAS_TILES, MOBA_BLOCK, MOBA_BLOCK), lambda b, g, p: (g, 0, 0, 0)),
        ],
        out_specs=tile,
        out_shape=jax.ShapeDtypeStruct((m, KEY_W), bf16),
        scratch_shapes=[pltpu.VMEM((n_units, MOBA_UNROLL, MOBA_BLOCK, MOBA_BLOCK), f32),
                        pltpu.VMEM((n_units, MOBA_UNROLL, 1, MOBA_BLOCK), f32),
                        pltpu.VMEM((n_units, MOBA_UNROLL, MOBA_BLOCK, MOBA_BLOCK), bf16)],
        compiler_params=_cparams(("parallel", "parallel", "arbitrary")),
        name=name,
    )(q, z, k, v_t, k_mean, bias)


def kernel(x, a_norm_g, a_w_in, a_conv_w, a_log, a_dt_bias, a_out_norm_g, a_w_out, kv_norm_g, w_kv, b_norm_g, b_w_in, b_w_out, rel_bias, final_norm_g):
    bsz, seq, d = x.shape
    m = bsz * seq
    assert d == D_MODEL and seq % MOBA_BLOCK == 0 and seq % GDN_TB == 0 and m % MM_TM == 0 and m % GATES_TM == 0
    xf = x.reshape(m, d).astype(f32)

    qkvz_w = CONV_CH + KEY_W
    for i in range(a_w_in.shape[0]):
        w_ab = jnp.pad(a_w_in[i, :, qkvz_w:], ((0, 0), (0, LANES - 2 * HEADS)))
        proj, ab = _gdn_in_proj(xf, a_norm_g[i], a_w_in, i, qkvz_w, w_ab, f"gdn{i}_in_proj")
        cols = _gdn_gates(ab, a_log[i], a_dt_bias[i])
        o = _gdn_core(proj, a_conv_w[i].astype(f32), cols, a_out_norm_g[i].astype(f32),
                      bsz, seq, f"gdn{i}_core")
        xf = _out_proj(o, a_w_out, i, xf, final_norm_g, False, f"gdn{i}_out_proj")

    k, v_t, k_mean = _kv_proj(xf, kv_norm_g, w_kv, bsz, seq)
    k_mean = k_mean.reshape(bsz, seq // MOBA_BLOCK, KEY_W)
    bias = _bias_tiles(rel_bias)

    n_b = b_w_in.shape[0]
    for j in range(n_b):
        q, z = _moba_in_proj(xf, b_norm_g[j], b_w_in, j, bsz, seq, f"moba{j}_in_proj")
        o = _moba_attention(q, z, k, v_t, k_mean, bias, bsz, seq, f"moba{j}_attn")
        xf = _out_proj(o, b_w_out, j, xf, final_norm_g, j == n_b - 1, f"moba{j}_out_proj")
    return xf.reshape(bsz, seq, d).astype(x.dtype)
```
